```python
import math
import jax, jax.numpy as jnp
from jax import lax
import numpy as np

D_MODEL = 1024
BATCH = 4
SEQ = 4096
DEPTH = 2

D_A = 512
D_B = 512
N_HEADS = 8
HEAD_DIM = 64
D_C = N_HEADS * HEAD_DIM
CONV_WIDTH = 31
CHUNK = 128
N_GROUPS_B = 4
GROUP_DIM_B = D_B // N_GROUPS_B
Q_BLOCK = 128
N_BRANCHES = 3
EPS = 1e-6
SPLIT_SIZES = (N_BRANCHES * D_MODEL, 2 * D_A, D_A, 2 * D_B, D_B, D_C, D_C, D_C, D_C, N_HEADS)
N_IN = N_BRANCHES * D_MODEL + 3 * D_A + 3 * D_B + 4 * D_C + N_HEADS

kernel_name = "hybrid_gated_conformer_gmlp_fox"


def rmsnorm(x, g):
    x32 = x.astype(jnp.float32)
    y = x32 * lax.rsqrt(jnp.mean(x32 * x32, axis=-1, keepdims=True) + EPS)
    return (y * g.astype(jnp.float32)).astype(x.dtype)


def layernorm(x, g, b):
    x32 = x.astype(jnp.float32)
    mu = jnp.mean(x32, axis=-1, keepdims=True)
    xc = x32 - mu
    y = xc * lax.rsqrt(jnp.mean(xc * xc, axis=-1, keepdims=True) + EPS)
    return (y * g.astype(jnp.float32) + b.astype(jnp.float32)).astype(x.dtype)


def split_columns(p):
    idx = []
    acc = 0
    for s in SPLIT_SIZES[:-1]:
        acc += s
        idx.append(acc)
    return jnp.split(p, idx, axis=-1)


def causal_depthwise_conv(a, w, b):
    out = lax.conv_general_dilated(
        a, w[:, None, :].astype(a.dtype), window_strides=(1,),
        padding=[(CONV_WIDTH - 1, 0)],
        dimension_numbers=("NWC", "WIO", "NWC"),
        feature_group_count=a.shape[-1])
    return out + b.astype(a.dtype)


def conformer_branch(a_in, a_z, conv_w, conv_b, cn_g, cn_b, w_a):
    a = a_in[..., :D_A] * jax.nn.sigmoid(a_in[..., D_A:])
    a = causal_depthwise_conv(a, conv_w, conv_b)
    a = jax.nn.silu(layernorm(a, cn_g, cn_b))
    a = a * jax.nn.silu(a_z)
    return a @ w_a


def gmlp_branch(b_uv, b_z, gn_g, w_s, b_s, w_b):
    bsz, seq, _ = b_uv.shape
    u, v = b_uv[..., :D_B], b_uv[..., D_B:]
    v = rmsnorm(v, gn_g)
    v = v.reshape(bsz, seq // CHUNK, CHUNK, N_GROUPS_B, GROUP_DIM_B)
    tril = jnp.tril(jnp.ones((CHUNK, CHUNK), dtype=bool))
    ws = jnp.where(tril[None], w_s, jnp.zeros_like(w_s))
    mixed = jnp.einsum("gts,bnsgc->bntgc", ws, v) + b_s.T[None, None, :, :, None]
    y = u * mixed.reshape(bsz, seq, D_B)
    y = y * jax.nn.silu(b_z)
    return y @ w_b


def fox_attention(q, k, v, logf):
    bsz, seq = q.shape[0], q.shape[1]
    n_blocks = seq // Q_BLOCK
    scale = 1.0 / math.sqrt(HEAD_DIM)
    c = jnp.cumsum(logf, axis=1).transpose(0, 2, 1)
    qb = q.reshape(bsz, n_blocks, Q_BLOCK, N_HEADS, HEAD_DIM).transpose(1, 0, 2, 3, 4)
    cb = c.reshape(bsz, N_HEADS, n_blocks, Q_BLOCK).transpose(2, 0, 1, 3)
    kpos = jnp.arange(seq)

    def one_block(args):
        i, qi, ci = args
        s = jnp.einsum("bqhd,bkhd->bhqk", qi, k).astype(jnp.float32) * scale
        s = s + ci[..., :, None] - c[:, :, None, :]
        qpos = i * Q_BLOCK + jnp.arange(Q_BLOCK)
        mask = kpos[None, :] <= qpos[:, None]
        s = jnp.where(mask[None, None], s, -jnp.inf)
        p = jax.nn.softmax(s, axis=-1)
        return jnp.einsum("bhqk,bkhd->bqhd", p.astype(v.dtype), v)

    out = lax.map(one_block, (jnp.arange(n_blocks), qb, cb))
    return out.transpose(1, 0, 2, 3, 4).reshape(bsz, seq, N_HEADS, HEAD_DIM)


def fox_branch(q, k, v, c_z, f_pre, qn_g, kn_g, b_f, w_c):
    bsz, seq, _ = q.shape
    q = rmsnorm(q.reshape(bsz, seq, N_HEADS, HEAD_DIM), qn_g)
    k = rmsnorm(k.reshape(bsz, seq, N_HEADS, HEAD_DIM), kn_g)
    v = v.reshape(bsz, seq, N_HEADS, HEAD_DIM)
    logf = jax.nn.log_sigmoid(f_pre.astype(jnp.float32) + b_f.astype(jnp.float32))
    o = fox_attention(q, k, v, logf).reshape(bsz, seq, D_C)
    o = o * jax.nn.silu(c_z)
    return o @ w_c


def setup_inputs(seed: int = 0) -> dict:
    key = jax.random.key(seed)
    ks = jax.random.split(key, 20)
    n = jax.random.normal
    L = DEPTH
    return {
        "x": n(ks[0], (BATCH, SEQ, D_MODEL), jnp.float32),
        "norm_g": 1.0 + 0.02 * n(ks[1], (L, D_MODEL), jnp.float32),
        "w_in": n(ks[2], (L, D_MODEL, N_IN), jnp.float32) * D_MODEL ** -0.5,
        "b_gate": 0.02 * n(ks[3], (L, N_BRANCHES * D_MODEL), jnp.float32),
        "conv_w": n(ks[4], (L, CONV_WIDTH, D_A), jnp.float32) * CONV_WIDTH ** -0.5,
        "conv_b": 0.02 * n(ks[5], (L, D_A), jnp.float32),
        "conv_norm_g": 1.0 + 0.02 * n(ks[6], (L, D_A), jnp.float32),
        "conv_norm_b": 0.02 * n(ks[7], (L, D_A), jnp.float32),
        "w_a": n(ks[8], (L, D_A, D_MODEL), jnp.float32) * D_A ** -0.5,
        "gmlp_norm_g": 1.0 + 0.02 * n(ks[9], (L, D_B), jnp.float32),
        "w_s": n(ks[10], (L, N_GROUPS_B, CHUNK, CHUNK), jnp.float32) * CHUNK ** -0.5,
        "b_s": 1.0 + 0.1 * n(ks[11], (L, N_GROUPS_B, CHUNK), jnp.float32),
        "w_b": n(ks[12], (L, D_B, D_MODEL), jnp.float32) * D_B ** -0.5,
        "q_norm_g": 1.0 + 0.02 * n(ks[13], (L, HEAD_DIM), jnp.float32),
        "k_norm_g": 1.0 + 0.02 * n(ks[14], (L, HEAD_DIM), jnp.float32),
        "b_f": 2.0 + 0.1 * n(ks[15], (L, N_HEADS), jnp.float32),
        "w_c": n(ks[16], (L, D_C, D_MODEL), jnp.float32) * D_C ** -0.5,
        "w_out": n(ks[17], (L, D_MODEL, D_MODEL), jnp.float32) * D_MODEL ** -0.5,
    }


def reference(x, norm_g, w_in, b_gate, conv_w, conv_b, conv_norm_g, conv_norm_b, w_a,
              gmlp_norm_g, w_s, b_s, w_b, q_norm_g, k_norm_g, b_f, w_c, w_out):
    for l in range(DEPTH):
        h = rmsnorm(x, norm_g[l])
        p = h @ w_in[l]
        gate_pre, a_in, a_z, b_uv, b_z, q, k, v, c_z, f_pre = split_columns(p)
        gates = jax.nn.sigmoid(gate_pre + b_gate[l])
        g_a = gates[..., :D_MODEL]
        g_b = gates[..., D_MODEL:2 * D_MODEL]
        g_c = gates[..., 2 * D_MODEL:]
        y_a = conformer_branch(a_in, a_z, conv_w[l], conv_b[l], conv_norm_g[l],
                               conv_norm_b[l], w_a[l])
        y_b = gmlp_branch(b_uv, b_z, gmlp_norm_g[l], w_s[l], b_s[l], w_b[l])
        y_c = fox_branch(q, k, v, c_z, f_pre, q_norm_g[l], k_norm_g[l], b_f[l], w_c[l])
        merged = g_a * y_a + g_b * y_b + g_c * y_c
        x = x + merged @ w_out[l]
    return x
```

```python
import functools
import math

import numpy as np
import jax
import jax.numpy as jnp
from jax import lax
from jax.experimental import pallas as pl
from jax.experimental.pallas import tpu as pltpu

D_MODEL = 1024
D_A = 512
D_B = 512
N_HEADS = 8
HEAD_DIM = 64
D_C = N_HEADS * HEAD_DIM
CONV_WIDTH = 31
CHUNK = 128
N_GROUPS_B = 4
EPS = 1e-6
N_IN = 3 * D_MODEL + 3 * D_A + 3 * D_B + 4 * D_C + N_HEADS

LANES = 128
N_IN_PAD = ((N_IN + LANES - 1) // LANES) * LANES
OFF_GATE = 0
OFF_A = 3 * D_MODEL
OFF_B = OFF_A + 3 * D_A
OFF_C = OFF_B + 3 * D_B
OFF_F = OFF_C + 4 * D_C

HALO = 32
TM = 512
TQ = 512
LOG2E = math.log2(math.e)
QK_SCALE = LOG2E / math.sqrt(HEAD_DIM)
MASK_VALUE = -1e30
VMEM_LIMIT = 56 * 1024 * 1024

F32 = jnp.float32
BF16 = jnp.bfloat16


def _dot(a, b):
    return jnp.dot(a, b, preferred_element_type=F32)


def _sigmoid(x):
    return 1.0 / (1.0 + jnp.exp(-x))


def _silu(x):
    return x * _sigmoid(x)


def _split3(x):
    hi = x.astype(BF16).astype(F32)
    r = x - hi
    mid = r.astype(BF16).astype(F32)
    lo = (r - mid).astype(BF16).astype(F32)
    return hi, mid, lo


def _head_sum_matrix():
    m = np.zeros((2 * D_C, LANES), np.float32)
    for h in range(N_HEADS):
        m[h * HEAD_DIM:(h + 1) * HEAD_DIM, h] = 1.0
        m[D_C + h * HEAD_DIM:D_C + (h + 1) * HEAD_DIM, N_HEADS + h] = 1.0
    return m


def _head_bcast_matrix():
    m = np.zeros((LANES, 2 * D_C), np.float32)
    for part in range(2):
        for h in range(N_HEADS):
            m[16 * part + h, h * HEAD_DIM:(h + 1) * HEAD_DIM] = 1.0
            m[16 * part + N_HEADS + h, D_C + h * HEAD_DIM:D_C + (h + 1) * HEAD_DIM] = 1.0
    return m


def _aug_offset(h):
    return HEAD_DIM if h % 2 == 0 else 0


def _decay_place_matrix():
    m = np.zeros((LANES, 2 * N_HEADS * LANES), np.float32)
    for h in range(N_HEADS):
        qb = h * LANES + _aug_offset(h)
        kb = N_HEADS * LANES + h * LANES + _aug_offset(h)
        for part in range(3):
            m[8 * part + h, qb + part] = 1.0
            m[24, qb + 3 + part] = 1.0
            m[24, kb + part] = 1.0
            m[8 * part + h, kb + 3 + part] = -1.0
    return m


def _proj_kernel(x_ref, ng_ref, win_ref, bg_ref, convw_ref, convb_ref, cng_ref, cnb_ref, wa_ref,
                 gng_ref, ws_ref, bst_ref, wb_ref, qkg_ref, bf_ref, esum_ref, ebc_ref, place_ref,
                 mab_ref, gc_ref, qp_ref, kp_ref, vp_ref, scz_ref,
                 abuf_ref, ccar_ref):
    i = pl.program_id(1)

    @pl.when(i == 0)
    def _():
        abuf_ref[0:HALO, :] = jnp.zeros((HALO, D_A), F32)
        ccar_ref[...] = jnp.zeros((1, LANES), F32)

    x = x_ref[0]
    ms = jnp.mean(x * x, axis=-1, keepdims=True)
    h = ((x * lax.rsqrt(ms + EPS)) * ng_ref[...]).astype(BF16)

    def gate(idx):
        lo = OFF_GATE + idx * D_MODEL
        return _sigmoid(_dot(h, win_ref[:, lo:lo + D_MODEL]) + bg_ref[:, idx * D_MODEL:(idx + 1) * D_MODEL])

    pa = _dot(h, win_ref[:, OFF_A:OFF_A + 3 * D_A])
    a = pa[:, :D_A] * _sigmoid(pa[:, D_A:2 * D_A])
    abuf_ref[HALO:HALO + TM, :] = a
    conv = jnp.broadcast_to(convb_ref[...], (TM, D_A))
    for j in range(CONV_WIDTH):
        conv = conv + convw_ref[j:j + 1, :] * abuf_ref[pl.ds(HALO - (CONV_WIDTH - 1) + j, TM), :]
    abuf_ref[0:HALO, :] = abuf_ref[TM:TM + HALO, :]
    mu = jnp.mean(conv, axis=-1, keepdims=True)
    xc = conv - mu
    var = jnp.mean(xc * xc, axis=-1, keepdims=True)
    ln = (xc * lax.rsqrt(var + EPS)) * cng_ref[...] + cnb_ref[...]
    act_a = (_silu(ln) * _silu(pa[:, 2 * D_A:])).astype(BF16)
    mab = gate(0) * _dot(act_a, wa_ref[...])

    pb = _dot(h, win_ref[:, OFF_B:OFF_B + 3 * D_B])
    u = pb[:, :D_B]
    v = pb[:, D_B:2 * D_B]
    v = (v * lax.rsqrt(jnp.mean(v * v, axis=-1, keepdims=True) + EPS)) * gng_ref[...]
    vb = v.astype(BF16)
    n_chunks = TM // CHUNK
    row = lax.broadcasted_iota(jnp.int32, (CHUNK, CHUNK), 0)
    col = lax.broadcasted_iota(jnp.int32, (CHUNK, CHUNK), 1)
    tril = col <= row
    mixed_g = []
    for g in range(N_GROUPS_B):
        gs = slice(g * CHUNK, (g + 1) * CHUNK)
        vg = jnp.concatenate([vb[n * CHUNK:(n + 1) * CHUNK, gs] for n in range(n_chunks)], axis=1)
        wsg = jnp.where(tril, ws_ref[g], 0.0).astype(BF16)
        mixed_g.append(_dot(wsg, vg) + bst_ref[:, g:g + 1])
    mixed = jnp.concatenate(
        [jnp.concatenate([mixed_g[g][:, n * CHUNK:(n + 1) * CHUNK] for g in range(N_GROUPS_B)], axis=1)
         for n in range(n_chunks)], axis=0)
    act_b = ((u * mixed) * _silu(pb[:, 2 * D_B:])).astype(BF16)
    mab = mab + gate(1) * _dot(act_b, wb_ref[...])
    mab_ref[0] = mab.astype(BF16)
    gc_ref[0] = gate(2).astype(BF16)

    pc = _dot(h, win_ref[:, OFF_C:OFF_F + LANES])
    qk = pc[:, :2 * D_C]
    vv = pc[:, 2 * D_C:3 * D_C]
    scz_ref[0] = _silu(pc[:, 3 * D_C:4 * D_C]).astype(BF16)
    lane = lax.broadcasted_iota(jnp.int32, (TM, LANES), 1)

    ssq = _dot((qk * qk).astype(BF16), esum_ref[...])
    rs = lax.rsqrt(ssq * (1.0 / HEAD_DIM) + EPS)
    rs_hi = rs.astype(BF16).astype(F32)
    rs_lo = (rs - rs_hi).astype(BF16).astype(F32)
    rs_split = jnp.where(lane < 16, rs_hi, jnp.where(lane < 32, pltpu.roll(rs_lo, 16, 1), 0.0))
    rs_b = _dot(rs_split.astype(BF16), ebc_ref[...])
    qkn = (qk * rs_b) * qkg_ref[...]

    z = pc[:, 4 * D_C:] + bf_ref[...]
    logf = jnp.minimum(z, 0.0) - jnp.log1p(jnp.exp(-jnp.abs(z)))
    logf = jnp.where(lane < N_HEADS, logf, 0.0)
    l_hi, l_mid, l_lo = _split3(logf)
    l_split = l_hi + pltpu.roll(l_mid, 8, 1) + pltpu.roll(l_lo, 16, 1)
    trow = lax.broadcasted_iota(jnp.int32, (TM, TM), 0)
    tcol = lax.broadcasted_iota(jnp.int32, (TM, TM), 1)
    tri = jnp.where(tcol <= trow, 1.0, 0.0).astype(BF16)
    r = _dot(tri, l_split.astype(BF16))
    c = r + pltpu.roll(r, LANES - 8, 1) + pltpu.roll(r, LANES - 16, 1)
    c = jnp.where(lane < N_HEADS, c, 0.0) + ccar_ref[...]
    ccar_ref[...] = c[TM - 1:TM, :]
    c_hi, c_mid, c_lo = _split3(c * LOG2E)
    c_split = c_hi + pltpu.roll(c_mid, 8, 1) + pltpu.roll(c_lo, 16, 1)
    c_split = jnp.where(lane == 24, 1.0, c_split)
    aug = _dot(c_split.astype(BF16), place_ref[...])

    for hd in range(N_HEADS):
        p = hd // 2
        own = (lane < HEAD_DIM) if hd % 2 == 0 else (lane >= HEAD_DIM)
        one_lane = HEAD_DIM if hd % 2 == 0 else 0
        q_pair = qkn[:, p * LANES:(p + 1) * LANES]
        k_pair = qkn[:, D_C + p * LANES:D_C + (p + 1) * LANES]
        v_pair = vv[:, p * LANES:(p + 1) * LANES]
        qp_ref[0, hd] = (jnp.where(own, q_pair, 0.0) + aug[:, hd * LANES:(hd + 1) * LANES]).astype(BF16)
        kp_ref[0, hd] = (jnp.where(own, k_pair, 0.0)
                         + aug[:, (N_HEADS + hd) * LANES:(N_HEADS + hd + 1) * LANES]).astype(BF16)
        vp_ref[0, hd] = jnp.where(own, v_pair, jnp.where(lane == one_lane, 1.0, 0.0)).astype(BF16)


def _const_spec(shape):
    nd = len(shape)
    return pl.BlockSpec(shape, lambda b, i, _nd=nd: (0,) * _nd, pipeline_mode=pl.Buffered(1))


def _proj_call(x, ng, win, bg, convw, convb, cng, cnb, wa, gng, ws, bst, wb, qkg, bf, esum, ebc, place):
    bsz, seq, _ = x.shape
    grid = (bsz, seq // TM)
    consts = (ng, win, bg, convw, convb, cng, cnb, wa, gng, ws, bst, wb, qkg, bf, esum, ebc, place)
    row_spec = lambda w: pl.BlockSpec((1, TM, w), lambda b, i: (b, i, 0))
    head_spec = pl.BlockSpec((1, N_HEADS, TM, LANES), lambda b, i: (b, 0, i, 0))
    head_shape = jax.ShapeDtypeStruct((bsz, N_HEADS, seq, LANES), BF16)
    return pl.pallas_call(
        _proj_kernel,
        grid=grid,
        in_specs=[row_spec(D_MODEL)] + [_const_spec(c.shape) for c in consts],
        out_specs=[row_spec(D_MODEL), row_spec(D_MODEL), head_spec, head_spec, head_spec, row_spec(D_C)],
        out_shape=[jax.ShapeDtypeStruct((bsz, seq, D_MODEL), BF16),
                   jax.ShapeDtypeStruct((bsz, seq, D_MODEL), BF16),
                   head_shape, head_shape, head_shape,
                   jax.ShapeDtypeStruct((bsz, seq, D_C), BF16)],
        scratch_shapes=[pltpu.VMEM((TM + HALO, D_A), F32), pltpu.VMEM((1, LANES), F32)],
        compiler_params=pltpu.CompilerParams(
            dimension_semantics=("arbitrary", "arbitrary"), vmem_limit_bytes=VMEM_LIMIT),
        name="proj",
    )(x, *consts)


def _attn_kernel(q_ref, k_ref, v_ref, o_ref):
    i = pl.program_id(2)
    qs = [q_ref[0, hh] for hh in range(2)]

    def step(j, carry, masked):
        out = []
        for hh in range(2):
            m, acc = carry[hh]
            kb = k_ref[0, hh, pl.ds(j * TQ, TQ), :]
            vb = v_ref[0, hh, pl.ds(j * TQ, TQ), :]
            s = lax.dot_general(qs[hh], kb, (((1,), (1,)), ((), ())), preferred_element_type=F32)
            if masked:
                qpos = lax.broadcasted_iota(jnp.int32, (TQ, TQ), 0)
                kpos = lax.broadcasted_iota(jnp.int32, (TQ, TQ), 1)
                s = jnp.where(kpos <= qpos, s, MASK_VALUE)
            m_new = jnp.maximum(m, jnp.max(s, axis=1, keepdims=True))
            alpha = jnp.exp2(m - m_new)
            p = jnp.exp2(s - m_new).astype(BF16)
            out.append((m_new, alpha * acc + _dot(p, vb)))
        return tuple(out)

    init = tuple((jnp.full((TQ, 1), MASK_VALUE, F32), jnp.zeros((TQ, LANES), F32)) for _ in range(2))
    carry = lax.fori_loop(0, i, lambda j, c: step(j, c, False), init)
    (_, acc0), (_, acc1) = step(i, carry, True)
    lane = lax.broadcasted_iota(jnp.int32, (TQ, LANES), 1)
    o = jnp.where(lane < HEAD_DIM, acc0 / acc0[:, HEAD_DIM:HEAD_DIM + 1], acc1 / acc1[:, 0:1])
    o_ref[0] = o.astype(BF16)


def _attn_call(qp, kp, vp):
    bsz, _, seq, _ = qp.shape
    grid = (bsz, N_HEADS // 2, seq // TQ)
    kv_spec = pl.BlockSpec((1, 2, seq, LANES), lambda b, p, i: (b, p, 0, 0))
    return pl.pallas_call(
        _attn_kernel,
        grid=grid,
        in_specs=[pl.BlockSpec((1, 2, TQ, LANES), lambda b, p, i: (b, p, i, 0)), kv_spec, kv_spec],
        out_specs=pl.BlockSpec((1, TQ, LANES), lambda b, p, i: (b, i, p)),
        out_shape=jax.ShapeDtypeStruct((bsz, seq, D_C), BF16),
        compiler_params=pltpu.CompilerParams(
            dimension_semantics=("arbitrary", "arbitrary", "arbitrary"), vmem_limit_bytes=VMEM_LIMIT),
        name="attn",
    )(qp, kp, vp)


def _merge_kernel(x_ref, mab_ref, gc_ref, o_ref, scz_ref, wc_ref, wout_ref, out_ref):
    act_c = (o_ref[0].astype(F32) * scz_ref[0].astype(F32)).astype(BF16)
    merged = mab_ref[0].astype(F32) + gc_ref[0].astype(F32) * _dot(act_c, wc_ref[...])
    out_ref[0] = x_ref[0] + _dot(merged.astype(BF16), wout_ref[...])


def _merge_call(x, mab, gc, o, scz, wc, wout):
    bsz, seq, _ = x.shape
    row_spec = lambda w: pl.BlockSpec((1, TM, w), lambda b, i: (b, i, 0))
    return pl.pallas_call(
        _merge_kernel,
        grid=(bsz, seq // TM),
        in_specs=[row_spec(D_MODEL), row_spec(D_MODEL), row_spec(D_MODEL), row_spec(D_C), row_spec(D_C),
                  _const_spec(wc.shape), _const_spec(wout.shape)],
        out_specs=row_spec(D_MODEL),
        out_shape=jax.ShapeDtypeStruct(x.shape, F32),
        compiler_params=pltpu.CompilerParams(
            dimension_semantics=("arbitrary", "arbitrary"), vmem_limit_bytes=VMEM_LIMIT),
        name="merge",
    )(x, mab, gc, o, scz, wc, wout)


def kernel(x, norm_g, w_in, b_gate, conv_w, conv_b, conv_norm_g, conv_norm_b, w_a, gmlp_norm_g,
           w_s, b_s, w_b, q_norm_g, k_norm_g, b_f, w_c, w_out):
    depth = w_in.shape[0]
    esum = jnp.asarray(_head_sum_matrix(), BF16)
    ebc = jnp.asarray(_head_bcast_matrix(), BF16)
    place = jnp.asarray(_decay_place_matrix(), BF16)
    win = jnp.pad(w_in.astype(BF16), ((0, 0), (0, 0), (0, N_IN_PAD - N_IN)))
    qkg = jnp.concatenate([jnp.tile(q_norm_g, (1, N_HEADS)) * QK_SCALE, jnp.tile(k_norm_g, (1, N_HEADS))], axis=1)
    bf = jnp.pad(b_f, ((0, 0), (0, LANES - N_HEADS)))
    row = lambda a, l: a[l][None, :]
    for l in range(depth):
        mab, gc, qp, kp, vp, scz = _proj_call(
            x, row(norm_g, l), win[l], row(b_gate, l), conv_w[l], row(conv_b, l), row(conv_norm_g, l),
            row(conv_norm_b, l), w_a[l].astype(BF16), row(gmlp_norm_g, l), w_s[l], b_s[l].T,
            w_b[l].astype(BF16), row(qkg, l), row(bf, l), esum, ebc, place)
        o = _attn_call(qp, kp, vp)
        x = _merge_call(x, mab, gc, o, scz, w_c[l].astype(BF16), w_out[l].astype(BF16))
    return x
```

```python
import math

import numpy as np
import jax
import jax.numpy as jnp
from jax import lax
from jax.experimental import pallas as pl
from jax.experimental.pallas import tpu as pltpu

D_MODEL = 1024
D_A = 512
D_B = 512
N_HEADS = 8
HEAD_DIM = 64
D_C = N_HEADS * HEAD_DIM
CONV_WIDTH = 31
CHUNK = 128
N_GROUPS_B = 4
EPS = 1e-6
N_IN = 3 * D_MODEL + 3 * D_A + 3 * D_B + 4 * D_C + N_HEADS

LANES = 128
SUBLANES = 8
MXU_COLS = 256
TILE_COLS = MXU_COLS
N_IN_PAD = ((N_IN + LANES - 1) // LANES) * LANES
OFF_GATE = 0
OFF_A = 3 * D_MODEL
OFF_B = OFF_A + 3 * D_A
OFF_C = OFF_B + 3 * D_B
OFF_F = OFF_C + 4 * D_C
W_HEAD = 3 * D_A + LANES
W_REST = 3 * D_MODEL + 3 * D_B + 4 * D_C
R_GATE = 0
R_B = R_GATE + 3 * D_MODEL
R_C = R_B + 3 * D_B

HALO = 32
CONV_ROWS = 64
TM = 256
TM_MERGE = 512
TQ = 1024
TK = 256
TQS = 256
MXU_TILE_COST = 130
CONV_UNIT_COST = 235
GATE_TILE_COST = 100
LOG2E = math.log2(math.e)
QK_SCALE = LOG2E / math.sqrt(HEAD_DIM)
MASK_VALUE = -1e30
VMEM_LIMIT = 58 * 1024 * 1024

F32 = jnp.float32
BF16 = jnp.bfloat16


def _dot(a, b):
    return jnp.dot(a, b, preferred_element_type=F32)


def _sigmoid(x):
    return 1.0 / (1.0 + jnp.exp2(x * (-LOG2E)))


def _silu(x):
    return x * _sigmoid(x)


def _split3(x):
    hi = x.astype(BF16).astype(F32)
    r = x - hi
    mid = r.astype(BF16).astype(F32)
    lo = (r - mid).astype(BF16).astype(F32)
    return hi, mid, lo


def _head_sum_matrix():
    m = np.zeros((2 * D_C, LANES), np.float32)
    for h in range(N_HEADS):
        m[h * HEAD_DIM:(h + 1) * HEAD_DIM, h] = 1.0
        m[D_C + h * HEAD_DIM:D_C + (h + 1) * HEAD_DIM, N_HEADS + h] = 1.0
    return m


def _head_bcast_matrix():
    m = np.zeros((LANES, 2 * D_C), np.float32)
    for part in range(2):
        for h in range(N_HEADS):
            m[16 * part + h, h * HEAD_DIM:(h + 1) * HEAD_DIM] = 1.0
            m[16 * part + N_HEADS + h, D_C + h * HEAD_DIM:D_C + (h + 1) * HEAD_DIM] = 1.0
    return m


def _aug_offset(h):
    return HEAD_DIM if h % 2 == 0 else 0


def _decay_place_matrix():
    m = np.zeros((LANES, 2 * N_HEADS * LANES), np.float32)
    for h in range(N_HEADS):
        qb = h * LANES + _aug_offset(h)
        kb = N_HEADS * LANES + h * LANES + _aug_offset(h)
        for part in range(3):
            m[8 * part + h, qb + part] = 1.0
            m[24, qb + 3 + part] = 1.0
            m[24, kb + part] = 1.0
            m[8 * part + h, kb + 3 + part] = -1.0
    return m


def _conv_unit(base, cb, abuf_ref, convw_ref):
    cs = slice(cb * LANES, (cb + 1) * LANES)
    win = abuf_ref[pl.ds(base, CONV_ROWS + HALO), cs]
    conv = None
    for r in range(SUBLANES):
        part = None
        rows = CONV_ROWS + (SUBLANES if r else 0)
        for j in range(CONV_WIDTH):
            off = HALO - (CONV_WIDTH - 1) + j
            if off % SUBLANES != r:
                continue
            term = convw_ref[j:j + 1, cs] * win[off - r:off - r + rows, :]
            part = term if part is None else part + term
        part = part[r:r + CONV_ROWS, :]
        conv = part if conv is None else conv + part
    return conv


def _conv_epilogue(units, a_z, convb_ref, cng_ref, cnb_ref):
    conv = jnp.concatenate(units, axis=1) + convb_ref[...]
    mu = jnp.mean(conv, axis=-1, keepdims=True)
    xc = conv - mu
    var = jnp.mean(xc * xc, axis=-1, keepdims=True)
    ln = (xc * lax.rsqrt(var + EPS)) * cng_ref[...] + cnb_ref[...]
    return (_silu(ln) * _silu(a_z)).astype(BF16)


def _emit_interleaved(mxu_tasks, valu_tasks):
    mi = vi = 0
    mcost = vcost = 0.0
    while mi < len(mxu_tasks) or vi < len(valu_tasks):
        take_mxu = vi == len(valu_tasks) or (
            mi < len(mxu_tasks) and (mcost <= vcost or valu_tasks[vi][2] > mi))
        if take_mxu:
            cost, fn = mxu_tasks[mi]
            mi += 1
            mcost += cost
            fn()
        else:
            cost, fn, _ = valu_tasks[vi]
            vi += 1
            vcost += cost
            fn()


def _proj_kernel(x_ref, ng_ref, win_ref, bg_ref, convw_ref, convb_ref, cng_ref, cnb_ref, wa_ref,
                 gng_ref, ws_ref, bst_ref, wb_ref, qkg_ref, bf_ref, esum_ref, ebc_ref, place_ref,
                 mab_ref, gc_ref, qp_ref, kp_ref, vp_ref, scz_ref,
                 abuf_ref, ccar_ref, h_ref, pa_ref, pbig_ref, act_ref):
    i = pl.program_id(1)

    @pl.when(i == 0)
    def _():
        abuf_ref[0:HALO, :] = jnp.zeros((HALO, D_A), F32)
        ccar_ref[...] = jnp.zeros((1, LANES), F32)

    x = x_ref[0]
    ms = jnp.mean(x * x, axis=-1, keepdims=True)
    h_ref[...] = ((x * lax.rsqrt(ms + EPS)) * ng_ref[...]).astype(BF16)

    pa_ref[:, 0:3 * D_A] = _dot(h_ref[...], win_ref[:, OFF_A:OFF_A + 3 * D_A])
    pa_ref[:, 3 * D_A:] = _dot(h_ref[...], win_ref[:, OFF_F:OFF_F + LANES])
    abuf_ref[HALO:HALO + TM, :] = pa_ref[:, 0:D_A] * _sigmoid(pa_ref[:, D_A:2 * D_A])

    def mxu_tile(t):
        def fn():
            lo = t * TILE_COLS
            src = lo if lo < R_B else lo + 3 * D_A
            pbig_ref[:, lo:lo + TILE_COLS] = _dot(h_ref[...], win_ref[:, src:src + TILE_COLS])
        return (MXU_TILE_COST, fn)

    def conv_tasks(sub):
        rows = sub * CONV_ROWS
        units = []

        def unit(cb):
            return lambda: units.append(_conv_unit(rows, cb, abuf_ref, convw_ref))

        def epilogue():
            act_ref[rows:rows + CONV_ROWS, :] = _conv_epilogue(
                units, pa_ref[rows:rows + CONV_ROWS, 2 * D_A:3 * D_A], convb_ref, cng_ref, cnb_ref)

        return [(CONV_UNIT_COST, unit(cb), 0) for cb in range(D_A // LANES)] + [(CONV_UNIT_COST, epilogue, 0)]

    def gate_task(t):
        def fn():
            cs = slice(t * TILE_COLS, (t + 1) * TILE_COLS)
            pbig_ref[:, cs] = _sigmoid(pbig_ref[:, cs] + bg_ref[:, cs])
        return (GATE_TILE_COST, fn, t + 1)

    assert R_B % TILE_COLS == 0 and W_REST % TILE_COLS == 0
    n_gate_tiles = 3 * D_MODEL // TILE_COLS
    valu_tasks = [task for sub in range(TM // CONV_ROWS) for task in conv_tasks(sub)]
    valu_tasks += [gate_task(t) for t in range(n_gate_tiles)]
    _emit_interleaved([mxu_tile(t) for t in range(W_REST // TILE_COLS)], valu_tasks)
    abuf_ref[0:HALO, :] = abuf_ref[TM:TM + HALO, :]

    def gate(idx):
        return pbig_ref[:, R_GATE + idx * D_MODEL:R_GATE + (idx + 1) * D_MODEL]

    lane = lax.broadcasted_iota(jnp.int32, (TM, LANES), 1)

    ya = _dot(act_ref[...], wa_ref[...])

    v = pbig_ref[:, R_B + D_B:R_B + 2 * D_B]
    v = (v * lax.rsqrt(jnp.mean(v * v, axis=-1, keepdims=True) + EPS)) * gng_ref[...]
    vb = v.astype(BF16)
    n_chunks = TM // CHUNK
    row = lax.broadcasted_iota(jnp.int32, (CHUNK, CHUNK), 0)
    col = lax.broadcasted_iota(jnp.int32, (CHUNK, CHUNK), 1)
    tril = col <= row
    mixed_g = []
    for g in range(N_GROUPS_B):
        gs = slice(g * CHUNK, (g + 1) * CHUNK)
        vg = jnp.concatenate([vb[n * CHUNK:(n + 1) * CHUNK, gs] for n in range(n_chunks)], axis=1)
        wsg = jnp.where(tril, ws_ref[g], 0.0).astype(BF16)
        mixed_g.append(_dot(wsg, vg) + bst_ref[:, g:g + 1])

    qk = pbig_ref[:, R_C:R_C + 2 * D_C]
    ssq = _dot((qk * qk).astype(BF16), esum_ref[...])

    z = pa_ref[:, 3 * D_A:] + bf_ref[...]
    logf = jnp.minimum(z, 0.0) - jnp.log1p(jnp.exp(-jnp.abs(z)))
    logf = jnp.where(lane < N_HEADS, logf, 0.0)
    l_hi, l_mid, l_lo = _split3(logf)
    l_split = l_hi + pltpu.roll(l_mid, 8, 1) + pltpu.roll(l_lo, 16, 1)
    trow = lax.broadcasted_iota(jnp.int32, (TM, TM), 0)
    tcol = lax.broadcasted_iota(jnp.int32, (TM, TM), 1)
    tri = jnp.where(tcol <= trow, 1.0, 0.0).astype(BF16)
    r = _dot(tri, l_split.astype(BF16))

    mab = gate(0) * ya

    rs = lax.rsqrt(ssq * (1.0 / HEAD_DIM) + EPS)
    rs_hi = rs.astype(BF16).astype(F32)
    rs_lo = (rs - rs_hi).astype(BF16).astype(F32)
    rs_split = jnp.where(lane < 16, rs_hi, jnp.where(lane < 32, pltpu.roll(rs_lo, 16, 1), 0.0))
    rs_b = _dot(rs_split.astype(BF16), ebc_ref[...])

    c = r + pltpu.roll(r, LANES - 8, 1) + pltpu.roll(r, LANES - 16, 1)
    c = jnp.where(lane < N_HEADS, c, 0.0) + ccar_ref[...]
    ccar_ref[...] = c[TM - 1:TM, :]
    c_hi, c_mid, c_lo = _split3(c * LOG2E)
    c_split = c_hi + pltpu.roll(c_mid, 8, 1) + pltpu.roll(c_lo, 16, 1)
    c_split = jnp.where(lane == 24, 1.0, c_split)
    aug = _dot(c_split.astype(BF16), place_ref[...])

    mixed = jnp.concatenate(
        [jnp.concatenate([mixed_g[g][:, n * CHUNK:(n + 1) * CHUNK] for g in range(N_GROUPS_B)], axis=1)
         for n in range(n_chunks)], axis=0)
    u = pbig_ref[:, R_B:R_B + D_B]
    act_b = ((u * mixed) * _silu(pbig_ref[:, R_B + 2 * D_B:R_B + 3 * D_B])).astype(BF16)
    yb = _dot(act_b, wb_ref[...])

    qkn = (qk * rs_b) * qkg_ref[...]
    vv = pbig_ref[:, R_C + 2 * D_C:R_C + 3 * D_C]
    scz_ref[0] = _silu(pbig_ref[:, R_C + 3 * D_C:R_C + 4 * D_C]).astype(BF16)
    gc_ref[0] = gate(2).astype(BF16)
    for hd in range(N_HEADS):
        p = hd // 2
        own = (lane < HEAD_DIM) if hd % 2 == 0 else (lane >= HEAD_DIM)
        one_lane = HEAD_DIM if hd % 2 == 0 else 0
        q_pair = qkn[:, p * LANES:(p + 1) * LANES]
        k_pair = qkn[:, D_C + p * LANES:D_C + (p + 1) * LANES]
        v_pair = vv[:, p * LANES:(p + 1) * LANES]
        qp_ref[0, hd] = (jnp.where(own, q_pair, 0.0) + aug[:, hd * LANES:(hd + 1) * LANES]).astype(BF16)
        kp_ref[0, hd] = (jnp.where(own, k_pair, 0.0)
                         + aug[:, (N_HEADS + hd) * LANES:(N_HEADS + hd + 1) * LANES]).astype(BF16)
        vp_ref[0, hd] = jnp.where(own, v_pair, jnp.where(lane == one_lane, 1.0, 0.0)).astype(BF16)

    mab_ref[0] = (mab + gate(1) * yb).astype(BF16)


def _const_spec(shape):
    nd = len(shape)
    return pl.BlockSpec(shape, lambda b, i, _nd=nd: (0,) * _nd, pipeline_mode=pl.Buffered(1))


def _proj_call(layer, x, ng, win, bg, convw, convb, cng, cnb, wa, gng, ws, bst, wb, qkg, bf, esum, ebc, place):
    bsz, seq, _ = x.shape
    grid = (bsz, seq // TM)
    consts = (ng, win, bg, convw, convb, cng, cnb, wa, gng, ws, bst, wb, qkg, bf, esum, ebc, place)
    const_specs = [_const_spec(c.shape) for c in consts]
    const_specs[1] = pl.BlockSpec((None,) + win.shape[1:], lambda b, i: (layer, 0, 0),
                                  pipeline_mode=pl.Buffered(1))
    row_spec = lambda w: pl.BlockSpec((1, TM, w), lambda b, i: (b, i, 0))
    head_spec = pl.BlockSpec((1, N_HEADS, TM, LANES), lambda b, i: (b, 0, i, 0))
    head_shape = jax.ShapeDtypeStruct((bsz, N_HEADS, seq, LANES), BF16)
    return pl.pallas_call(
        _proj_kernel,
        grid=grid,
        in_specs=[row_spec(D_MODEL)] + const_specs,
        out_specs=[row_spec(D_MODEL), row_spec(D_MODEL), head_spec, head_spec, head_spec, row_spec(D_C)],
        out_shape=[jax.ShapeDtypeStruct((bsz, seq, D_MODEL), BF16),
                   jax.ShapeDtypeStruct((bsz, seq, D_MODEL), BF16),
                   head_shape, head_shape, head_shape,
                   jax.ShapeDtypeStruct((bsz, seq, D_C), BF16)],
        scratch_shapes=[pltpu.VMEM((TM + HALO, D_A), F32), pltpu.VMEM((1, LANES), F32),
                        pltpu.VMEM((TM, D_MODEL), BF16), pltpu.VMEM((TM, W_HEAD), F32),
                        pltpu.VMEM((TM, W_REST), F32),
                        pltpu.VMEM((TM, D_A), BF16)],
        compiler_params=pltpu.CompilerParams(
            dimension_semantics=("arbitrary", "arbitrary"), vmem_limit_bytes=VMEM_LIMIT),
        name="proj",
    )(x, *consts)


def _attn_kernel(q_ref, k_ref, v_ref, o_ref, sa_ref, sb_ref):
    i = pl.program_id(2)
    units = [(hh, st) for hh in range(2) for st in range(TQ // TQS)]
    qs = [q_ref[0, hh, st * TQS:(st + 1) * TQS, :] for hh, st in units]

    per_tile = TQ // TK
    assert per_tile % 2 == 0

    def key_rows(blk):
        return pl.ds(pl.multiple_of(blk * TK, TK), TK)

    all_units = tuple(range(len(units)))

    def qk(blk, s_ref, active=all_units):
        for u in active:
            s_ref[u] = lax.dot_general(k_ref[0, units[u][0], key_rows(blk), :], qs[u],
                                       (((1,), (1,)), ((), ())), preferred_element_type=F32)

    def softmax_pv(blk, s_ref, carry, active=all_units, diag=None):
        def scores(u):
            s = s_ref[u]
            if diag is not None:
                kpos = lax.broadcasted_iota(jnp.int32, (TK, TQS), 0) + diag * TK
                qpos = lax.broadcasted_iota(jnp.int32, (TK, TQS), 1) + units[u][1] * TQS
                s = jnp.where(kpos <= qpos, s, MASK_VALUE)
            return s

        m_new = {u: jnp.maximum(carry[u][0], jnp.max(scores(u), axis=0, keepdims=True)) for u in active}
        probs = {u: jnp.exp2(scores(u) - m_new[u]).astype(BF16) for u in active}
        out = list(carry)
        for u in active:
            pv = lax.dot_general(v_ref[0, units[u][0], key_rows(blk), :], probs[u],
                                 (((0,), (0,)), ((), ())), preferred_element_type=F32)
            out[u] = (m_new[u], jnp.exp2(carry[u][0] - m_new[u]) * carry[u][1] + pv)
        return tuple(out)

    carry = tuple((jnp.full((1, TQS), MASK_VALUE, F32), jnp.zeros((LANES, TQS), F32)) for _ in units)
    qk(0, sa_ref)

    def pair(p, carry):
        qk(2 * p + 1, sb_ref)
        carry = softmax_pv(2 * p, sa_ref, carry)
        qk(2 * p + 2, sa_ref)
        return softmax_pv(2 * p + 1, sb_ref, carry)

    carry = lax.fori_loop(0, i * (per_tile // 2), pair, carry)
    bufs = (sa_ref, sb_ref)
    seeing = [tuple(u for u in all_units if (units[u][1] + 1) * TQS > d * TK) for d in range(per_tile)]
    for d in range(per_tile):
        if d + 1 < per_tile:
            qk(i * per_tile + d + 1, bufs[(d + 1) % 2], seeing[d + 1])
        carry = softmax_pv(i * per_tile + d, bufs[d % 2], carry, seeing[d], diag=d)

    n_st = TQ // TQS
    acc0 = jnp.concatenate([carry[st][1] for st in range(n_st)], axis=1)
    acc1 = jnp.concatenate([carry[n_st + st][1] for st in range(n_st)], axis=1)
    feat = lax.broadcasted_iota(jnp.int32, (LANES, TQ), 0)
    o_t = jnp.where(feat < HEAD_DIM, acc0 / acc0[HEAD_DIM:HEAD_DIM + 1, :], acc1 / acc1[0:1, :])
    o_ref[0] = o_t.T.astype(BF16)


def _attn_call(qp, kp, vp):
    bsz, _, seq, _ = qp.shape
    grid = (bsz, N_HEADS // 2, seq // TQ)
    kv_spec = pl.BlockSpec((1, 2, seq, LANES), lambda b, p, i: (b, p, 0, 0))
    return pl.pallas_call(
        _attn_kernel,
        grid=grid,
        in_specs=[pl.BlockSpec((1, 2, TQ, LANES), lambda b, p, i: (b, p, i, 0)), kv_spec, kv_spec],
        out_specs=pl.BlockSpec((1, TQ, LANES), lambda b, p, i: (b, i, p)),
        out_shape=jax.ShapeDtypeStruct((bsz, seq, D_C), BF16),
        scratch_shapes=[pltpu.VMEM((2 * (TQ // TQS), TK, TQS), F32) for _ in range(2)],
        compiler_params=pltpu.CompilerParams(
            dimension_semantics=("arbitrary", "arbitrary", "arbitrary"), vmem_limit_bytes=VMEM_LIMIT),
        name="attn",
    )(qp, kp, vp)


def _merge_kernel(x_ref, mab_ref, gc_ref, o_ref, scz_ref, wc_ref, wout_ref, out_ref):
    act_c = (o_ref[0].astype(F32) * scz_ref[0].astype(F32)).astype(BF16)
    merged = mab_ref[0].astype(F32) + gc_ref[0].astype(F32) * _dot(act_c, wc_ref[...])
    out_ref[0] = x_ref[0] + _dot(merged.astype(BF16), wout_ref[...])


def _merge_call(x, mab, gc, o, scz, wc, wout):
    bsz, seq, _ = x.shape
    row_spec = lambda w: pl.BlockSpec((1, TM_MERGE, w), lambda b, i: (b, i, 0))
    return pl.pallas_call(
        _merge_kernel,
        grid=(bsz, seq // TM_MERGE),
        in_specs=[row_spec(D_MODEL), row_spec(D_MODEL), row_spec(D_MODEL), row_spec(D_C), row_spec(D_C),
                  _const_spec(wc.shape), _const_spec(wout.shape)],
        out_specs=row_spec(D_MODEL),
        out_shape=jax.ShapeDtypeStruct(x.shape, F32),
        compiler_params=pltpu.CompilerParams(
            dimension_semantics=("arbitrary", "arbitrary"), vmem_limit_bytes=VMEM_LIMIT),
        name="merge",
    )(x, mab, gc, o, scz, wc, wout)


def kernel(x, norm_g, w_in, b_gate, conv_w, conv_b, conv_norm_g, conv_norm_b, w_a, gmlp_norm_g,
           w_s, b_s, w_b, q_norm_g, k_norm_g, b_f, w_c, w_out):
    depth = w_in.shape[0]
    esum = jnp.asarray(_head_sum_matrix(), BF16)
    ebc = jnp.asarray(_head_bcast_matrix(), BF16)
    place = jnp.asarray(_decay_place_matrix(), BF16)
    win = jnp.pad(w_in.astype(BF16), ((0, 0), (0, 0), (0, N_IN_PAD - N_IN)))
    qkg = jnp.concatenate([jnp.tile(q_norm_g, (1, N_HEADS)) * QK_SCALE, jnp.tile(k_norm_g, (1, N_HEADS))], axis=1)
    bf = jnp.pad(b_f, ((0, 0), (0, LANES - N_HEADS)))
    row = lambda a, l: a[l][None, :]
    for l in range(depth):
        mab, gc, qp, kp, vp, scz = _proj_call(
            l, x, row(norm_g, l), win, row(b_gate, l), conv_w[l], row(conv_b, l), row(conv_norm_g, l),
            row(conv_norm_b, l), w_a[l].astype(BF16), row(gmlp_norm_g, l), w_s[l], b_s[l].T,
            w_b[l].astype(BF16), row(qkg, l), row(bf, l), esum, ebc, place)
        o = _attn_call(qp, kp, vp)
        x = _merge_call(x, mab, gc, o, scz, w_c[l].astype(BF16), w_out[l].astype(BF16))
    return x
```

```python
import math

import numpy as np
import jax
import jax.numpy as jnp
from jax import lax
from jax.experimental import pallas as pl
from jax.experimental.pallas import tpu as pltpu

D_MODEL = 1024
D_A = 512
D_B = 512
N_HEADS = 8
HEAD_DIM = 64
D_C = N_HEADS * HEAD_DIM
CONV_WIDTH = 31
CHUNK = 128
N_GROUPS_B = 4
EPS = 1e-6
N_IN = 3 * D_MODEL + 3 * D_A + 3 * D_B + 4 * D_C + N_HEADS

LANES = 128
SUBLANES = 8
MXU_COLS = 256
TILE_COLS = MXU_COLS
OFF_GATE = 0
OFF_A = 3 * D_MODEL
OFF_B = OFF_A + 3 * D_A
OFF_C = OFF_B + 3 * D_B
OFF_F = OFF_C + 4 * D_C
W_HEAD = 3 * D_A + LANES
W_REST = 3 * D_MODEL + 3 * D_B + 4 * D_C
R_GATE = 0
R_B = R_GATE + 3 * D_MODEL
R_C = R_B + 3 * D_B

HALO = 32
CONV_ROWS = 64
TM = 256
TM_MERGE = 1024
TQ = 1024
TK = 256
TQS = 256
MXU_TILE_COST = 130
CONV_UNIT_COST = 235
GATE_TILE_COST = 100
LOG2E = math.log2(math.e)
QK_SCALE = LOG2E / math.sqrt(HEAD_DIM)
MASK_VALUE = -1e30
VMEM_LIMIT = 58 * 1024 * 1024

F32 = jnp.float32
BF16 = jnp.bfloat16


def _dot(a, b):
    return jnp.dot(a, b, preferred_element_type=F32)


def _sigmoid(x):
    return 1.0 / (1.0 + jnp.exp2(x * (-LOG2E)))


def _silu(x):
    return x * _sigmoid(x)


def _split3(x):
    hi = x.astype(BF16).astype(F32)
    r = x - hi
    mid = r.astype(BF16).astype(F32)
    lo = (r - mid).astype(BF16).astype(F32)
    return hi, mid, lo


def _head_sum_matrix():
    m = np.zeros((2 * D_C, LANES), np.float32)
    for h in range(N_HEADS):
        m[h * HEAD_DIM:(h + 1) * HEAD_DIM, h] = 1.0
        m[D_C + h * HEAD_DIM:D_C + (h + 1) * HEAD_DIM, N_HEADS + h] = 1.0
    return m


def _head_bcast_matrix():
    m = np.zeros((LANES, 2 * D_C), np.float32)
    for part in range(2):
        for h in range(N_HEADS):
            m[16 * part + h, h * HEAD_DIM:(h + 1) * HEAD_DIM] = 1.0
            m[16 * part + N_HEADS + h, D_C + h * HEAD_DIM:D_C + (h + 1) * HEAD_DIM] = 1.0
    return m


def _aug_offset(h):
    return HEAD_DIM if h % 2 == 0 else 0


def _decay_place_matrix():
    m = np.zeros((LANES, 2 * N_HEADS * LANES), np.float32)
    for h in range(N_HEADS):
        qb = h * LANES + _aug_offset(h)
        kb = N_HEADS * LANES + h * LANES + _aug_offset(h)
        for part in range(3):
            m[8 * part + h, qb + part] = 1.0
            m[24, qb + 3 + part] = 1.0
            m[24, kb + part] = 1.0
            m[8 * part + h, kb + 3 + part] = -1.0
    return m


def _zero_after(token):
    bits = pltpu.bitcast(token, jnp.uint32)
    return lax.shift_right_logical(lax.shift_right_logical(bits, jnp.uint32(16)), jnp.uint32(16)).astype(F32)


def _conv_unit(base, cb, abuf_ref, convw_ref, token):
    cs = slice(cb * LANES, (cb + 1) * LANES)
    win = abuf_ref[pl.ds(base, CONV_ROWS + HALO), cs]
    win = jnp.concatenate([win[:SUBLANES] + _zero_after(token), win[SUBLANES:]], axis=0)
    conv = None
    for r in range(SUBLANES):
        part = None
        rows = CONV_ROWS + (SUBLANES if r else 0)
        for j in range(CONV_WIDTH):
            off = HALO - (CONV_WIDTH - 1) + j
            if off % SUBLANES != r:
                continue
            term = convw_ref[j:j + 1, cs] * win[off - r:off - r + rows, :]
            part = term if part is None else part + term
        part = part[r:r + CONV_ROWS, :]
        conv = part if conv is None else conv + part
    return conv


def _conv_epilogue(units, a_z, convb_ref, cng_ref, cnb_ref):
    conv = jnp.concatenate(units, axis=1) + convb_ref[...]
    mu = jnp.mean(conv, axis=-1, keepdims=True)
    xc = conv - mu
    var = jnp.mean(xc * xc, axis=-1, keepdims=True)
    ln = (xc * lax.rsqrt(var + EPS)) * cng_ref[...] + cnb_ref[...]
    return (_silu(ln) * _silu(a_z)).astype(BF16)


def _emit_interleaved(mxu_tasks, valu_tasks):
    mi = vi = 0
    mcost = vcost = 0.0
    while mi < len(mxu_tasks) or vi < len(valu_tasks):
        take_mxu = vi == len(valu_tasks) or (
            mi < len(mxu_tasks) and (mcost <= vcost or valu_tasks[vi][2] > mi))
        if take_mxu:
            cost, fn = mxu_tasks[mi]
            mi += 1
            mcost += cost
            fn()
        else:
            cost, fn, _ = valu_tasks[vi]
            vi += 1
            vcost += cost
            fn()


def _proj_kernel(x_ref, ng_ref, win_ref, bg_ref, convw_ref, convb_ref, cng_ref, cnb_ref, wa_ref,
                 gng_ref, ws_ref, bst_ref, wb_ref, qkg_ref, bf_ref, esum_ref, ebc_ref, place_ref,
                 mab_ref, gc_ref, qp_ref, kp_ref, vp_ref, scz_ref,
                 abuf_ref, ccar_ref, h_ref, pa_ref, pbig_ref, act_ref):
    i = pl.program_id(1)

    @pl.when(i == 0)
    def _():
        abuf_ref[0:HALO, :] = jnp.zeros((HALO, D_A), F32)
        ccar_ref[...] = jnp.zeros((1, LANES), F32)
        pa_ref[:, 3 * D_A:] = jnp.zeros((TM, LANES), F32)

    x = x_ref[0]
    ms = jnp.mean(x * x, axis=-1, keepdims=True)
    h_ref[...] = ((x * lax.rsqrt(ms + EPS)) * ng_ref[...]).astype(BF16)

    pa_ref[:, 0:3 * D_A] = _dot(h_ref[...], win_ref[:, OFF_A:OFF_A + 3 * D_A])
    pa_ref[:, 3 * D_A:3 * D_A + N_HEADS] = _dot(h_ref[...], win_ref[:, OFF_F:OFF_F + N_HEADS])
    abuf_ref[HALO:HALO + TM, :] = pa_ref[:, 0:D_A] * _sigmoid(pa_ref[:, D_A:2 * D_A])

    def mxu_tile(t):
        def fn():
            lo = t * TILE_COLS
            src = lo if lo < R_B else lo + 3 * D_A
            res = _dot(h_ref[...], win_ref[:, src:src + TILE_COLS])
            pbig_ref[:, lo:lo + TILE_COLS] = res
            tokens[t] = res[:SUBLANES, :LANES]
        return (MXU_TILE_COST, fn)

    tokens = {}
    n_tiles = W_REST // TILE_COLS
    n_units = (TM // CONV_ROWS) * (D_A // LANES)

    def conv_tasks(sub):
        rows = sub * CONV_ROWS
        units = []

        def unit(cb):
            k = sub * (D_A // LANES) + cb
            tile = min(n_tiles - 1, (k * n_tiles) // n_units)
            fn = lambda: units.append(_conv_unit(rows, cb, abuf_ref, convw_ref, tokens[tile]))
            return (CONV_UNIT_COST, fn, tile + 1)

        def epilogue():
            act_ref[rows:rows + CONV_ROWS, :] = _conv_epilogue(
                units, pa_ref[rows:rows + CONV_ROWS, 2 * D_A:3 * D_A], convb_ref, cng_ref, cnb_ref)

        return [unit(cb) for cb in range(D_A // LANES)] + [(CONV_UNIT_COST, epilogue, 0)]

    def gate_task(t):
        def fn():
            cs = slice(t * TILE_COLS, (t + 1) * TILE_COLS)
            pbig_ref[:, cs] = _sigmoid(pbig_ref[:, cs] + bg_ref[:, cs])
        return (GATE_TILE_COST, fn, t + 1)

    assert R_B % TILE_COLS == 0 and W_REST % TILE_COLS == 0
    n_gate_tiles = 3 * D_MODEL // TILE_COLS
    valu_tasks = [task for sub in range(TM // CONV_ROWS) for task in conv_tasks(sub)]
    valu_tasks += [gate_task(t) for t in range(n_gate_tiles)]
    _emit_interleaved([mxu_tile(t) for t in range(W_REST // TILE_COLS)], valu_tasks)
    abuf_ref[0:HALO, :] = abuf_ref[TM:TM + HALO, :]

    def gate(idx):
        return pbig_ref[:, R_GATE + idx * D_MODEL:R_GATE + (idx + 1) * D_MODEL]

    lane = lax.broadcasted_iota(jnp.int32, (TM, LANES), 1)

    ya = _dot(act_ref[...], wa_ref[...])

    v = pbig_ref[:, R_B + D_B:R_B + 2 * D_B]
    v = (v * lax.rsqrt(jnp.mean(v * v, axis=-1, keepdims=True) + EPS)) * gng_ref[...]
    vb = v.astype(BF16)
    n_chunks = TM // CHUNK
    row = lax.broadcasted_iota(jnp.int32, (CHUNK, CHUNK), 0)
    col = lax.broadcasted_iota(jnp.int32, (CHUNK, CHUNK), 1)
    tril = col <= row
    mixed_g = []
    for g in range(N_GROUPS_B):
        gs = slice(g * CHUNK, (g + 1) * CHUNK)
        vg = jnp.concatenate([vb[n * CHUNK:(n + 1) * CHUNK, gs] for n in range(n_chunks)], axis=1)
        wsg = jnp.where(tril, ws_ref[g], 0.0).astype(BF16)
        mixed_g.append(_dot(wsg, vg) + bst_ref[:, g:g + 1])

    qk = pbig_ref[:, R_C:R_C + 2 * D_C]
    ssq = _dot((qk * qk).astype(BF16), esum_ref[...])

    z = pa_ref[:, 3 * D_A:] + bf_ref[...]
    logf = jnp.minimum(z, 0.0) - jnp.log1p(jnp.exp(-jnp.abs(z)))
    logf = jnp.where(lane < N_HEADS, logf, 0.0)
    l_hi, l_mid, l_lo = _split3(logf)
    l_split = l_hi + pltpu.roll(l_mid, 8, 1) + pltpu.roll(l_lo, 16, 1)
    trow = lax.broadcasted_iota(jnp.int32, (TM, TM), 0)
    tcol = lax.broadcasted_iota(jnp.int32, (TM, TM), 1)
    tri = jnp.where(tcol <= trow, 1.0, 0.0).astype(BF16)
    r = _dot(tri, l_split.astype(BF16))

    mab = gate(0) * ya

    rs = lax.rsqrt(ssq * (1.0 / HEAD_DIM) + EPS)
    rs_hi = rs.astype(BF16).astype(F32)
    rs_lo = (rs - rs_hi).astype(BF16).astype(F32)
    rs_split = jnp.where(lane < 16, rs_hi, jnp.where(lane < 32, pltpu.roll(rs_lo, 16, 1), 0.0))
    rs_b = _dot(rs_split.astype(BF16), ebc_ref[...])

    c = r + pltpu.roll(r, LANES - 8, 1) + pltpu.roll(r, LANES - 16, 1)
    c = jnp.where(lane < N_HEADS, c, 0.0) + ccar_ref[...]
    ccar_ref[...] = c[TM - 1:TM, :]
    c_hi, c_mid, c_lo = _split3(c * LOG2E)
    c_split = c_hi + pltpu.roll(c_mid, 8, 1) + pltpu.roll(c_lo, 16, 1)
    c_split = jnp.where(lane == 24, 1.0, c_split)
    aug = _dot(c_split.astype(BF16), place_ref[...])

    mixed = jnp.concatenate(
        [jnp.concatenate([mixed_g[g][:, n * CHUNK:(n + 1) * CHUNK] for g in range(N_GROUPS_B)], axis=1)
         for n in range(n_chunks)], axis=0)
    u = pbig_ref[:, R_B:R_B + D_B]
    act_b = ((u * mixed) * _silu(pbig_ref[:, R_B + 2 * D_B:R_B + 3 * D_B])).astype(BF16)
    yb = _dot(act_b, wb_ref[...])

    qkn = (qk * rs_b) * qkg_ref[...]
    vv = pbig_ref[:, R_C + 2 * D_C:R_C + 3 * D_C]
    scz_ref[0] = _silu(pbig_ref[:, R_C + 3 * D_C:R_C + 4 * D_C]).astype(BF16)
    gc_ref[0] = gate(2).astype(BF16)
    for hd in range(N_HEADS):
        p = hd // 2
        own = (lane < HEAD_DIM) if hd % 2 == 0 else (lane >= HEAD_DIM)
        one_lane = HEAD_DIM if hd % 2 == 0 else 0
        q_pair = qkn[:, p * LANES:(p + 1) * LANES]
        k_pair = qkn[:, D_C + p * LANES:D_C + (p + 1) * LANES]
        v_pair = vv[:, p * LANES:(p + 1) * LANES]
        qp_ref[0, hd] = (jnp.where(own, q_pair, 0.0) + aug[:, hd * LANES:(hd + 1) * LANES]).astype(BF16)
        kp_ref[0, hd] = (jnp.where(own, k_pair, 0.0)
                         + aug[:, (N_HEADS + hd) * LANES:(N_HEADS + hd + 1) * LANES]).astype(BF16)
        vp_ref[0, hd] = jnp.where(own, v_pair, jnp.where(lane == one_lane, 1.0, 0.0)).astype(BF16)

    mab_ref[0] = (mab + gate(1) * yb).astype(BF16)


def _const_spec(shape):
    nd = len(shape)
    return pl.BlockSpec(shape, lambda b, i, _nd=nd: (0,) * _nd, pipeline_mode=pl.Buffered(1))


def _proj_call(layer, x, ng, win, bg, convw, convb, cng, cnb, wa, gng, ws, bst, wb, qkg, bf, esum, ebc, place):
    bsz, seq, _ = x.shape
    grid = (bsz, seq // TM)
    consts = (ng, win, bg, convw, convb, cng, cnb, wa, gng, ws, bst, wb, qkg, bf, esum, ebc, place)
    const_specs = [_const_spec(c.shape) for c in consts]
    const_specs[1] = pl.BlockSpec((None,) + win.shape[1:], lambda b, i: (layer, 0, 0),
                                  pipeline_mode=pl.Buffered(1))
    row_spec = lambda w: pl.BlockSpec((1, TM, w), lambda b, i: (b, i, 0))
    head_spec = pl.BlockSpec((1, N_HEADS, TM, LANES), lambda b, i: (b, 0, i, 0))
    head_shape = jax.ShapeDtypeStruct((bsz, N_HEADS, seq, LANES), BF16)
    return pl.pallas_call(
        _proj_kernel,
        grid=grid,
        in_specs=[row_spec(D_MODEL)] + const_specs,
        out_specs=[row_spec(D_MODEL), row_spec(D_MODEL), head_spec, head_spec, head_spec, row_spec(D_C)],
        out_shape=[jax.ShapeDtypeStruct((bsz, seq, D_MODEL), BF16),
                   jax.ShapeDtypeStruct((bsz, seq, D_MODEL), BF16),
                   head_shape, head_shape, head_shape,
                   jax.ShapeDtypeStruct((bsz, seq, D_C), BF16)],
        scratch_shapes=[pltpu.VMEM((TM + HALO, D_A), F32), pltpu.VMEM((1, LANES), F32),
                        pltpu.VMEM((TM, D_MODEL), BF16), pltpu.VMEM((TM, W_HEAD), F32),
                        pltpu.VMEM((TM, W_REST), F32),
                        pltpu.VMEM((TM, D_A), BF16)],
        compiler_params=pltpu.CompilerParams(
            dimension_semantics=("arbitrary", "arbitrary"), vmem_limit_bytes=VMEM_LIMIT),
        name="proj",
    )(x, *consts)


def _attn_kernel(q_ref, k_ref, v_ref, o_ref, sa_ref, sb_ref):
    i = pl.program_id(2)
    units = [(hh, st) for hh in range(2) for st in range(TQ // TQS)]
    qs = [q_ref[0, hh, st * TQS:(st + 1) * TQS, :] for hh, st in units]

    per_tile = TQ // TK
    assert per_tile % 2 == 0

    def key_rows(blk):
        return pl.ds(pl.multiple_of(blk * TK, TK), TK)

    all_units = tuple(range(len(units)))

    def qk(blk, s_ref, active=all_units):
        for u in active:
            s_ref[u] = lax.dot_general(k_ref[0, units[u][0], key_rows(blk), :], qs[u],
                                       (((1,), (1,)), ((), ())), preferred_element_type=F32)

    def softmax_pv(blk, s_ref, carry, active=all_units, diag=None):
        def scores(u):
            s = s_ref[u]
            if diag is not None:
                kpos = lax.broadcasted_iota(jnp.int32, (TK, TQS), 0) + diag * TK
                qpos = lax.broadcasted_iota(jnp.int32, (TK, TQS), 1) + units[u][1] * TQS
                s = jnp.where(kpos <= qpos, s, MASK_VALUE)
            return s

        m_new = {u: jnp.maximum(carry[u][0], jnp.max(scores(u), axis=0, keepdims=True)) for u in active}
        probs = {u: jnp.exp2(scores(u) - m_new[u]).astype(BF16) for u in active}
        out = list(carry)
        for u in active:
            pv = lax.dot_general(v_ref[0, units[u][0], key_rows(blk), :], probs[u],
                                 (((0,), (0,)), ((), ())), preferred_element_type=F32)
            out[u] = (m_new[u], jnp.exp2(carry[u][0] - m_new[u]) * carry[u][1] + pv)
        return tuple(out)

    carry = tuple((jnp.full((1, TQS), MASK_VALUE, F32), jnp.zeros((LANES, TQS), F32)) for _ in units)
    qk(0, sa_ref)

    def pair(p, carry):
        qk(2 * p + 1, sb_ref)
        carry = softmax_pv(2 * p, sa_ref, carry)
        qk(2 * p + 2, sa_ref)
        return softmax_pv(2 * p + 1, sb_ref, carry)

    carry = lax.fori_loop(0, i * (per_tile // 2), pair, carry)
    bufs = (sa_ref, sb_ref)
    seeing = [tuple(u for u in all_units if (units[u][1] + 1) * TQS > d * TK) for d in range(per_tile)]
    for d in range(per_tile):
        if d + 1 < per_tile:
            qk(i * per_tile + d + 1, bufs[(d + 1) % 2], seeing[d + 1])
        carry = softmax_pv(i * per_tile + d, bufs[d % 2], carry, seeing[d], diag=d)

    n_st = TQ // TQS
    acc0 = jnp.concatenate([carry[st][1] for st in range(n_st)], axis=1)
    acc1 = jnp.concatenate([carry[n_st + st][1] for st in range(n_st)], axis=1)
    feat = lax.broadcasted_iota(jnp.int32, (LANES, TQ), 0)
    o_t = jnp.where(feat < HEAD_DIM, acc0 / acc0[HEAD_DIM:HEAD_DIM + 1, :], acc1 / acc1[0:1, :])
    o_ref[0] = o_t.T.astype(BF16)


def _attn_call(qp, kp, vp):
    bsz, _, seq, _ = qp.shape
    grid = (bsz, N_HEADS // 2, seq // TQ)
    kv_spec = pl.BlockSpec((1, 2, seq, LANES), lambda b, p, i: (b, p, 0, 0))
    return pl.pallas_call(
        _attn_kernel,
        grid=grid,
        in_specs=[pl.BlockSpec((1, 2, TQ, LANES), lambda b, p, i: (b, p, i, 0)), kv_spec, kv_spec],
        out_specs=pl.BlockSpec((1, TQ, LANES), lambda b, p, i: (b, i, p)),
        out_shape=jax.ShapeDtypeStruct((bsz, seq, D_C), BF16),
        scratch_shapes=[pltpu.VMEM((2 * (TQ // TQS), TK, TQS), F32) for _ in range(2)],
        compiler_params=pltpu.CompilerParams(
            dimension_semantics=("arbitrary", "arbitrary", "arbitrary"), vmem_limit_bytes=VMEM_LIMIT),
        name="attn",
    )(qp, kp, vp)


def _merge_kernel(x_ref, mab_ref, gc_ref, o_ref, scz_ref, wc_ref, wout_ref, out_ref):
    act_c = (o_ref[0].astype(F32) * scz_ref[0].astype(F32)).astype(BF16)
    merged = mab_ref[0].astype(F32) + gc_ref[0].astype(F32) * _dot(act_c, wc_ref[...])
    out_ref[0] = x_ref[0] + _dot(merged.astype(BF16), wout_ref[...])


def _merge_call(x, mab, gc, o, scz, wc, wout):
    bsz, seq, _ = x.shape
    row_spec = lambda w: pl.BlockSpec((1, TM_MERGE, w), lambda b, i: (b, i, 0))
    return pl.pallas_call(
        _merge_kernel,
        grid=(bsz, seq // TM_MERGE),
        in_specs=[row_spec(D_MODEL), row_spec(D_MODEL), row_spec(D_MODEL), row_spec(D_C), row_spec(D_C),
                  _const_spec(wc.shape), _const_spec(wout.shape)],
        out_specs=row_spec(D_MODEL),
        out_shape=jax.ShapeDtypeStruct(x.shape, F32),
        compiler_params=pltpu.CompilerParams(
            dimension_semantics=("arbitrary", "arbitrary"), vmem_limit_bytes=VMEM_LIMIT),
        name="merge",
    )(x, mab, gc, o, scz, wc, wout)


def kernel(x, norm_g, w_in, b_gate, conv_w, conv_b, conv_norm_g, conv_norm_b, w_a, gmlp_norm_g,
           w_s, b_s, w_b, q_norm_g, k_norm_g, b_f, w_c, w_out):
    depth = w_in.shape[0]
    esum = jnp.asarray(_head_sum_matrix(), BF16)
    ebc = jnp.asarray(_head_bcast_matrix(), BF16)
    place = jnp.asarray(_decay_place_matrix(), BF16)
    win = w_in.astype(BF16)
    qkg = jnp.concatenate([jnp.tile(q_norm_g, (1, N_HEADS)) * QK_SCALE, jnp.tile(k_norm_g, (1, N_HEADS))], axis=1)
    bf = jnp.pad(b_f, ((0, 0), (0, LANES - N_HEADS)))
    row = lambda a, l: a[l][None, :]
    for l in range(depth):
        mab, gc, qp, kp, vp, scz = _proj_call(
            l, x, row(norm_g, l), win, row(b_gate, l), conv_w[l], row(conv_b, l), row(conv_norm_g, l),
            row(conv_norm_b, l), w_a[l].astype(BF16), row(gmlp_norm_g, l), w_s[l], b_s[l].T,
            w_b[l].astype(BF16), row(qkg, l), row(bf, l), esum, ebc, place)
        o = _attn_call(qp, kp, vp)
        x = _merge_call(x, mab, gc, o, scz, w_c[l].astype(BF16), w_out[l].astype(BF16))
    return x
```

```python
import functools
import math

import numpy as np
import jax
import jax.numpy as jnp
from jax import lax
from jax.experimental import pallas as pl
from jax.experimental.pallas import tpu as pltpu

D_MODEL = 1024
D_A = 512
D_B = 512
N_HEADS = 8
HEAD_DIM = 64
D_C = N_HEADS * HEAD_DIM
CONV_WIDTH = 31
CHUNK = 128
N_GROUPS_B = 4
EPS = 1e-6
N_IN = 3 * D_MODEL + 3 * D_A + 3 * D_B + 4 * D_C + N_HEADS

LANES = 128
SUBLANES = 8
MXU_COLS = 256
TILE_COLS = MXU_COLS
OFF_GATE = 0
OFF_A = 3 * D_MODEL
OFF_B = OFF_A + 3 * D_A
OFF_C = OFF_B + 3 * D_B
OFF_F = OFF_C + 4 * D_C
W_HEAD = 3 * D_A + LANES
W_REST = 3 * D_MODEL + 3 * D_B + 4 * D_C
R_GATE = 0
R_B = R_GATE + 3 * D_MODEL
R_C = R_B + 3 * D_B

HALO = 32
CONV_ROWS = 64
TM = 256
TM_MERGE = 1024
TQ = 1024
TK = 256
TQS = 256
MXU_TILE_COST = 130
CONV_UNIT_COST = 235
GATE_TILE_COST = 100
LOG2E = math.log2(math.e)
QK_SCALE = LOG2E / math.sqrt(HEAD_DIM)
MASK_VALUE = -1e30
VMEM_LIMIT = 58 * 1024 * 1024

F32 = jnp.float32
BF16 = jnp.bfloat16


def _dot(a, b):
    return jnp.dot(a, b, preferred_element_type=F32)


def _sigmoid(x):
    return 1.0 / (1.0 + jnp.exp2(x * (-LOG2E)))


def _silu(x):
    return x * _sigmoid(x)


def _split3(x):
    hi = x.astype(BF16).astype(F32)
    r = x - hi
    mid = r.astype(BF16).astype(F32)
    lo = (r - mid).astype(BF16).astype(F32)
    return hi, mid, lo


def _head_sum_matrix():
    m = np.zeros((2 * D_C, LANES), np.float32)
    for h in range(N_HEADS):
        m[h * HEAD_DIM:(h + 1) * HEAD_DIM, h] = 1.0
        m[D_C + h * HEAD_DIM:D_C + (h + 1) * HEAD_DIM, N_HEADS + h] = 1.0
    return m


def _head_bcast_matrix():
    m = np.zeros((LANES, 2 * D_C), np.float32)
    for part in range(2):
        for h in range(N_HEADS):
            m[16 * part + h, h * HEAD_DIM:(h + 1) * HEAD_DIM] = 1.0
            m[16 * part + N_HEADS + h, D_C + h * HEAD_DIM:D_C + (h + 1) * HEAD_DIM] = 1.0
    return m


def _aug_offset(h):
    return HEAD_DIM if h % 2 == 0 else 0


def _decay_place_matrix():
    m = np.zeros((LANES, 2 * N_HEADS * LANES), np.float32)
    for h in range(N_HEADS):
        qb = h * LANES + _aug_offset(h)
        kb = N_HEADS * LANES + h * LANES + _aug_offset(h)
        for part in range(3):
            m[8 * part + h, qb + part] = 1.0
            m[24, qb + 3 + part] = 1.0
            m[24, kb + part] = 1.0
            m[8 * part + h, kb + 3 + part] = -1.0
    return m


def _zero_after(token):
    bits = pltpu.bitcast(token, jnp.uint32)
    return lax.shift_right_logical(lax.shift_right_logical(bits, jnp.uint32(16)), jnp.uint32(16)).astype(F32)


def _conv_unit(base, cb, abuf_ref, convw_ref, token):
    cs = slice(cb * LANES, (cb + 1) * LANES)
    win = abuf_ref[pl.ds(base, CONV_ROWS + HALO), cs]
    win = jnp.concatenate([win[:SUBLANES] + _zero_after(token), win[SUBLANES:]], axis=0)
    conv = None
    for r in range(SUBLANES):
        part = None
        rows = CONV_ROWS + (SUBLANES if r else 0)
        for j in range(CONV_WIDTH):
            off = HALO - (CONV_WIDTH - 1) + j
            if off % SUBLANES != r:
                continue
            term = convw_ref[j:j + 1, cs] * win[off - r:off - r + rows, :]
            part = term if part is None else part + term
        part = part[r:r + CONV_ROWS, :]
        conv = part if conv is None else conv + part
    return conv


def _conv_epilogue(units, a_z, convb_ref, cng_ref, cnb_ref):
    conv = jnp.concatenate(units, axis=1) + convb_ref[...]
    mu = jnp.mean(conv, axis=-1, keepdims=True)
    xc = conv - mu
    var = jnp.mean(xc * xc, axis=-1, keepdims=True)
    ln = (xc * lax.rsqrt(var + EPS)) * cng_ref[...] + cnb_ref[...]
    return (_silu(ln) * _silu(a_z)).astype(BF16)


def _emit_interleaved(mxu_tasks, valu_tasks):
    mi = vi = 0
    mcost = vcost = 0.0
    while mi < len(mxu_tasks) or vi < len(valu_tasks):
        take_mxu = vi == len(valu_tasks) or (
            mi < len(mxu_tasks) and (mcost <= vcost or valu_tasks[vi][2] > mi))
        if take_mxu:
            cost, fn = mxu_tasks[mi]
            mi += 1
            mcost += cost
            fn()
        else:
            cost, fn, _ = valu_tasks[vi]
            vi += 1
            vcost += cost
            fn()


def _proj_kernel(x_ref, ng_ref, win_ref, wf_ref, bg_ref, convw_ref, convb_ref, cng_ref, cnb_ref, wa_ref,
                 gng_ref, ws_ref, bst_ref, wb_ref, qkg_ref, bf_ref, esum_ref, ebc_ref, place_ref,
                 mab_ref, gc_ref, qp_ref, kp_ref, vp_ref, scz_ref,
                 abuf_ref, ccar_ref, h_ref, pa_ref, pbig_ref, act_ref):
    i = pl.program_id(1)

    @pl.when(i == 0)
    def _():
        abuf_ref[0:HALO, :] = jnp.zeros((HALO, D_A), F32)
        ccar_ref[...] = jnp.zeros((1, LANES), F32)
        pa_ref[:, 3 * D_A:] = jnp.zeros((TM, LANES), F32)

    x = x_ref[0]
    ms = jnp.mean(x * x, axis=-1, keepdims=True)
    h_ref[...] = ((x * lax.rsqrt(ms + EPS)) * ng_ref[...]).astype(BF16)

    pa_ref[:, 0:3 * D_A] = _dot(h_ref[...], win_ref[:, OFF_A:OFF_A + 3 * D_A])
    pa_ref[:, 3 * D_A:3 * D_A + N_HEADS] = _dot(h_ref[...], wf_ref[...])
    abuf_ref[HALO:HALO + TM, :] = pa_ref[:, 0:D_A] * _sigmoid(pa_ref[:, D_A:2 * D_A])

    def mxu_tile(t):
        def fn():
            lo = t * TILE_COLS
            src = lo if lo < R_B else lo + 3 * D_A
            res = _dot(h_ref[...], win_ref[:, src:src + TILE_COLS])
            pbig_ref[:, lo:lo + TILE_COLS] = res
            tokens[t] = res[:SUBLANES, :LANES]
        return (MXU_TILE_COST, fn)

    tokens = {}
    n_tiles = W_REST // TILE_COLS
    n_units = (TM // CONV_ROWS) * (D_A // LANES)

    def conv_tasks(sub):
        rows = sub * CONV_ROWS
        units = []

        def unit(cb):
            k = sub * (D_A // LANES) + cb
            tile = min(n_tiles - 1, (k * n_tiles) // n_units)
            fn = lambda: units.append(_conv_unit(rows, cb, abuf_ref, convw_ref, tokens[tile]))
            return (CONV_UNIT_COST, fn, tile + 1)

        def epilogue():
            act_ref[rows:rows + CONV_ROWS, :] = _conv_epilogue(
                units, pa_ref[rows:rows + CONV_ROWS, 2 * D_A:3 * D_A], convb_ref, cng_ref, cnb_ref)

        return [unit(cb) for cb in range(D_A // LANES)] + [(CONV_UNIT_COST, epilogue, 0)]

    def gate_task(t):
        def fn():
            cs = slice(t * TILE_COLS, (t + 1) * TILE_COLS)
            pbig_ref[:, cs] = _sigmoid(pbig_ref[:, cs] + bg_ref[:, cs])
        return (GATE_TILE_COST, fn, t + 1)

    assert R_B % TILE_COLS == 0 and W_REST % TILE_COLS == 0
    n_gate_tiles = 3 * D_MODEL // TILE_COLS
    valu_tasks = [task for sub in range(TM // CONV_ROWS) for task in conv_tasks(sub)]
    valu_tasks += [gate_task(t) for t in range(n_gate_tiles)]
    _emit_interleaved([mxu_tile(t) for t in range(W_REST // TILE_COLS)], valu_tasks)
    abuf_ref[0:HALO, :] = abuf_ref[TM:TM + HALO, :]

    def gate(idx):
        return pbig_ref[:, R_GATE + idx * D_MODEL:R_GATE + (idx + 1) * D_MODEL]

    lane = lax.broadcasted_iota(jnp.int32, (TM, LANES), 1)

    ya = _dot(act_ref[...], wa_ref[...])

    v = pbig_ref[:, R_B + D_B:R_B + 2 * D_B]
    v = (v * lax.rsqrt(jnp.mean(v * v, axis=-1, keepdims=True) + EPS)) * gng_ref[...]
    vb = v.astype(BF16)
    n_chunks = TM // CHUNK
    row = lax.broadcasted_iota(jnp.int32, (CHUNK, CHUNK), 0)
    col = lax.broadcasted_iota(jnp.int32, (CHUNK, CHUNK), 1)
    tril = col <= row
    mixed_g = []
    for g in range(N_GROUPS_B):
        gs = slice(g * CHUNK, (g + 1) * CHUNK)
        vg = jnp.concatenate([vb[n * CHUNK:(n + 1) * CHUNK, gs] for n in range(n_chunks)], axis=1)
        wsg = jnp.where(tril, ws_ref[g], 0.0).astype(BF16)
        mixed_g.append(_dot(wsg, vg) + bst_ref[:, g:g + 1])

    qk = pbig_ref[:, R_C:R_C + 2 * D_C]
    ssq = _dot((qk * qk).astype(BF16), esum_ref[...])

    z = pa_ref[:, 3 * D_A:] + bf_ref[...]
    logf = jnp.minimum(z, 0.0) - jnp.log1p(jnp.exp(-jnp.abs(z)))
    logf = jnp.where(lane < N_HEADS, logf, 0.0)
    l_hi, l_mid, l_lo = _split3(logf)
    l_split = l_hi + pltpu.roll(l_mid, 8, 1) + pltpu.roll(l_lo, 16, 1)
    trow = lax.broadcasted_iota(jnp.int32, (TM, TM), 0)
    tcol = lax.broadcasted_iota(jnp.int32, (TM, TM), 1)
    tri = jnp.where(tcol <= trow, 1.0, 0.0).astype(BF16)
    r = _dot(tri, l_split.astype(BF16))

    mab = gate(0) * ya

    rs = lax.rsqrt(ssq * (1.0 / HEAD_DIM) + EPS)
    rs_hi = rs.astype(BF16).astype(F32)
    rs_lo = (rs - rs_hi).astype(BF16).astype(F32)
    rs_split = jnp.where(lane < 16, rs_hi, jnp.where(lane < 32, pltpu.roll(rs_lo, 16, 1), 0.0))
    rs_b = _dot(rs_split.astype(BF16), ebc_ref[...])

    c = r + pltpu.roll(r, LANES - 8, 1) + pltpu.roll(r, LANES - 16, 1)
    c = jnp.where(lane < N_HEADS, c, 0.0) + ccar_ref[...]
    ccar_ref[...] = c[TM - 1:TM, :]
    c_hi, c_mid, c_lo = _split3(c * LOG2E)
    c_split = c_hi + pltpu.roll(c_mid, 8, 1) + pltpu.roll(c_lo, 16, 1)
    c_split = jnp.where(lane == 24, 1.0, c_split)
    aug = _dot(c_split.astype(BF16), place_ref[...])

    mixed = jnp.concatenate(
        [jnp.concatenate([mixed_g[g][:, n * CHUNK:(n + 1) * CHUNK] for g in range(N_GROUPS_B)], axis=1)
         for n in range(n_chunks)], axis=0)
    u = pbig_ref[:, R_B:R_B + D_B]
    act_b = ((u * mixed) * _silu(pbig_ref[:, R_B + 2 * D_B:R_B + 3 * D_B])).astype(BF16)
    yb = _dot(act_b, wb_ref[...])

    qkn = (qk * rs_b) * qkg_ref[...]
    vv = pbig_ref[:, R_C + 2 * D_C:R_C + 3 * D_C]
    scz_ref[0] = _silu(pbig_ref[:, R_C + 3 * D_C:R_C + 4 * D_C]).astype(BF16)
    gc_ref[0] = gate(2).astype(BF16)
    for hd in range(N_HEADS):
        p = hd // 2
        own = (lane < HEAD_DIM) if hd % 2 == 0 else (lane >= HEAD_DIM)
        one_lane = HEAD_DIM if hd % 2 == 0 else 0
        q_pair = qkn[:, p * LANES:(p + 1) * LANES]
        k_pair = qkn[:, D_C + p * LANES:D_C + (p + 1) * LANES]
        v_pair = vv[:, p * LANES:(p + 1) * LANES]
        qp_ref[0, hd] = (jnp.where(own, q_pair, 0.0) + aug[:, hd * LANES:(hd + 1) * LANES]).astype(BF16)
        kp_ref[0, hd] = (jnp.where(own, k_pair, 0.0)
                         + aug[:, (N_HEADS + hd) * LANES:(N_HEADS + hd + 1) * LANES]).astype(BF16)
        vp_ref[0, hd] = jnp.where(own, v_pair, jnp.where(lane == one_lane, 1.0, 0.0)).astype(BF16)

    mab_ref[0] = (mab + gate(1) * yb).astype(BF16)


def _const_spec(shape):
    nd = len(shape)
    return pl.BlockSpec(shape, lambda b, i, _nd=nd: (0,) * _nd, pipeline_mode=pl.Buffered(1))


def _proj_call(layer, x, ng, win, wf, bg, convw, convb, cng, cnb, wa, gng, ws, bst, wb, qkg, bf, esum, ebc, place):
    bsz, seq, _ = x.shape
    grid = (bsz, seq // TM)
    consts = (ng, win, wf, bg, convw, convb, cng, cnb, wa, gng, ws, bst, wb, qkg, bf, esum, ebc, place)
    const_specs = [_const_spec(c.shape) for c in consts]
    const_specs[1] = pl.BlockSpec((None,) + win.shape[1:], lambda b, i: (layer, 0, 0),
                                  pipeline_mode=pl.Buffered(1))
    row_spec = lambda w: pl.BlockSpec((1, TM, w), lambda b, i: (b, i, 0))
    head_spec = pl.BlockSpec((1, N_HEADS, TM, LANES), lambda b, i: (b, 0, i, 0))
    head_shape = jax.ShapeDtypeStruct((bsz, N_HEADS, seq, LANES), BF16)
    return pl.pallas_call(
        _proj_kernel,
        grid=grid,
        in_specs=[row_spec(D_MODEL)] + const_specs,
        out_specs=[row_spec(D_MODEL), row_spec(D_MODEL), head_spec, head_spec, head_spec, row_spec(D_C)],
        out_shape=[jax.ShapeDtypeStruct((bsz, seq, D_MODEL), BF16),
                   jax.ShapeDtypeStruct((bsz, seq, D_MODEL), BF16),
                   head_shape, head_shape, head_shape,
                   jax.ShapeDtypeStruct((bsz, seq, D_C), BF16)],
        scratch_shapes=[pltpu.VMEM((TM + HALO, D_A), F32), pltpu.VMEM((1, LANES), F32),
                        pltpu.VMEM((TM, D_MODEL), BF16), pltpu.VMEM((TM, W_HEAD), F32),
                        pltpu.VMEM((TM, W_REST), F32),
                        pltpu.VMEM((TM, D_A), BF16)],
        compiler_params=pltpu.CompilerParams(
            dimension_semantics=("arbitrary", "arbitrary"), vmem_limit_bytes=VMEM_LIMIT),
        name="proj",
    )(x, *consts)


def _attn_kernel(q_ref, k_ref, v_ref, o_ref, sa_ref, sb_ref):
    i = pl.program_id(2)
    units = [(hh, st) for hh in range(2) for st in range(TQ // TQS)]
    qs = [q_ref[0, hh, st * TQS:(st + 1) * TQS, :] for hh, st in units]

    per_tile = TQ // TK
    all_units = tuple(range(len(units)))
    seeing = [tuple(u for u in all_units if (units[u][1] + 1) * TQS > d * TK) for d in range(per_tile)]
    bufs = (sa_ref, sb_ref)

    def key_rows(blk):
        return slice(blk * TK, (blk + 1) * TK)

    def qk(blk, s_ref, active):
        for u in active:
            s_ref[u] = lax.dot_general(k_ref[0, units[u][0], key_rows(blk), :], qs[u],
                                       (((1,), (1,)), ((), ())), preferred_element_type=F32)

    def softmax_pv(blk, s_ref, carry, active, diag):
        def scores(u):
            s = s_ref[u]
            if diag is not None:
                kpos = lax.broadcasted_iota(jnp.int32, (TK, TQS), 0) + diag * TK
                qpos = lax.broadcasted_iota(jnp.int32, (TK, TQS), 1) + units[u][1] * TQS
                s = jnp.where(kpos <= qpos, s, MASK_VALUE)
            return s

        m_new = {u: jnp.maximum(carry[u][0], jnp.max(scores(u), axis=0, keepdims=True)) for u in active}
        probs = {u: jnp.exp2(scores(u) - m_new[u]).astype(BF16) for u in active}
        out = list(carry)
        for u in active:
            pv = lax.dot_general(v_ref[0, units[u][0], key_rows(blk), :], probs[u],
                                 (((0,), (0,)), ((), ())), preferred_element_type=F32)
            out[u] = (m_new[u], jnp.exp2(carry[u][0] - m_new[u]) * carry[u][1] + pv)
        return tuple(out)

    def tile(n_full):
        def users(blk):
            return all_units if blk < n_full else seeing[blk - n_full]

        n_blocks = n_full + per_tile
        carry = tuple((jnp.full((1, TQS), MASK_VALUE, F32), jnp.zeros((LANES, TQS), F32)) for _ in units)
        qk(0, bufs[0], users(0))
        for blk in range(n_blocks):
            if blk + 1 < n_blocks:
                qk(blk + 1, bufs[(blk + 1) % 2], users(blk + 1))
            carry = softmax_pv(blk, bufs[blk % 2], carry, users(blk), blk - n_full if blk >= n_full else None)

        n_st = TQ // TQS
        acc0 = jnp.concatenate([carry[st][1] for st in range(n_st)], axis=1)
        acc1 = jnp.concatenate([carry[n_st + st][1] for st in range(n_st)], axis=1)
        feat = lax.broadcasted_iota(jnp.int32, (LANES, TQ), 0)
        o_t = jnp.where(feat < HEAD_DIM, acc0 / acc0[HEAD_DIM:HEAD_DIM + 1, :], acc1 / acc1[0:1, :])
        o_ref[0] = o_t.T.astype(BF16)

    for case in range(k_ref.shape[2] // TQ):
        pl.when(i == case)(functools.partial(tile, case * per_tile))


def _attn_call(qp, kp, vp):
    bsz, _, seq, _ = qp.shape
    grid = (bsz, N_HEADS // 2, seq // TQ)
    kv_spec = pl.BlockSpec((1, 2, seq, LANES), lambda b, p, i: (b, p, 0, 0))
    return pl.pallas_call(
        _attn_kernel,
        grid=grid,
        in_specs=[pl.BlockSpec((1, 2, TQ, LANES), lambda b, p, i: (b, p, i, 0)), kv_spec, kv_spec],
        out_specs=pl.BlockSpec((1, TQ, LANES), lambda b, p, i: (b, i, p)),
        out_shape=jax.ShapeDtypeStruct((bsz, seq, D_C), BF16),
        scratch_shapes=[pltpu.VMEM((2 * (TQ // TQS), TK, TQS), F32) for _ in range(2)],
        compiler_params=pltpu.CompilerParams(
            dimension_semantics=("arbitrary", "arbitrary", "arbitrary"), vmem_limit_bytes=VMEM_LIMIT),
        name="attn",
    )(qp, kp, vp)


def _merge_kernel(x_ref, mab_ref, gc_ref, o_ref, scz_ref, wc_ref, wout_ref, out_ref):
    act_c = (o_ref[0].astype(F32) * scz_ref[0].astype(F32)).astype(BF16)
    merged = mab_ref[0].astype(F32) + gc_ref[0].astype(F32) * _dot(act_c, wc_ref[...])
    out_ref[0] = x_ref[0] + _dot(merged.astype(BF16), wout_ref[...])


def _merge_call(x, mab, gc, o, scz, wc, wout):
    bsz, seq, _ = x.shape
    row_spec = lambda w: pl.BlockSpec((1, TM_MERGE, w), lambda b, i: (b, i, 0))
    return pl.pallas_call(
        _merge_kernel,
        grid=(bsz, seq // TM_MERGE),
        in_specs=[row_spec(D_MODEL), row_spec(D_MODEL), row_spec(D_MODEL), row_spec(D_C), row_spec(D_C),
                  _const_spec(wc.shape), _const_spec(wout.shape)],
        out_specs=row_spec(D_MODEL),
        out_shape=jax.ShapeDtypeStruct(x.shape, F32),
        compiler_params=pltpu.CompilerParams(
            dimension_semantics=("arbitrary", "arbitrary"), vmem_limit_bytes=VMEM_LIMIT),
        name="merge",
    )(x, mab, gc, o, scz, wc, wout)


def kernel(x, norm_g, w_in, b_gate, conv_w, conv_b, conv_norm_g, conv_norm_b, w_a, gmlp_norm_g,
           w_s, b_s, w_b, q_norm_g, k_norm_g, b_f, w_c, w_out):
    depth = w_in.shape[0]
    esum = jnp.asarray(_head_sum_matrix(), BF16)
    ebc = jnp.asarray(_head_bcast_matrix(), BF16)
    place = jnp.asarray(_decay_place_matrix(), BF16)
    win = w_in[:, :, :OFF_F].astype(BF16)
    wf = w_in[:, :, OFF_F:].astype(BF16)
    qkg = jnp.concatenate([jnp.tile(q_norm_g, (1, N_HEADS)) * QK_SCALE, jnp.tile(k_norm_g, (1, N_HEADS))], axis=1)
    bf = jnp.pad(b_f, ((0, 0), (0, LANES - N_HEADS)))
    row = lambda a, l: a[l][None, :]
    for l in range(depth):
        mab, gc, qp, kp, vp, scz = _proj_call(
            l, x, row(norm_g, l), win, wf[l], row(b_gate, l), conv_w[l], row(conv_b, l), row(conv_norm_g, l),
            row(conv_norm_b, l), w_a[l].astype(BF16), row(gmlp_norm_g, l), w_s[l], b_s[l].T,
            w_b[l].astype(BF16), row(qkg, l), row(bf, l), esum, ebc, place)
        o = _attn_call(qp, kp, vp)
        x = _merge_call(x, mab, gc, o, scz, w_c[l].astype(BF16), w_out[l].astype(BF16))
    return x
```

```python
import functools
import math

import numpy as np
import jax
import jax.numpy as jnp
from jax import lax
from jax.experimental import pallas as pl
from jax.experimental.pallas import tpu as pltpu

D_MODEL = 1024
D_A = 512
D_B = 512
N_HEADS = 8
HEAD_DIM = 64
D_C = N_HEADS * HEAD_DIM
CONV_WIDTH = 31
CHUNK = 128
N_GROUPS_B = 4
EPS = 1e-6
N_IN = 3 * D_MODEL + 3 * D_A + 3 * D_B + 4 * D_C + N_HEADS

LANES = 128
SUBLANES = 8
MXU_COLS = 256
N_IN_PAD = ((N_IN + LANES - 1) // LANES) * LANES
CAST_COLS = 512
TILE_COLS = MXU_COLS
OFF_GATE = 0
OFF_A = 3 * D_MODEL
OFF_B = OFF_A + 3 * D_A
OFF_C = OFF_B + 3 * D_B
OFF_F = OFF_C + 4 * D_C
W_HEAD = 3 * D_A + LANES
W_REST = 3 * D_MODEL + 3 * D_B + 4 * D_C
R_GATE = 0
R_B = R_GATE + 3 * D_MODEL
R_C = R_B + 3 * D_B

HALO = 32
CONV_ROWS = 64
TM = 256
TM_MERGE = 1024
TQ = 1024
TK = 256
TQS = 256
MXU_TILE_COST = 130
CONV_UNIT_COST = 235
GATE_TILE_COST = 100
LOG2E = math.log2(math.e)
QK_SCALE = LOG2E / math.sqrt(HEAD_DIM)
MASK_VALUE = -1e30
VMEM_LIMIT = 58 * 1024 * 1024

F32 = jnp.float32
BF16 = jnp.bfloat16


def _dot(a, b):
    return jnp.dot(a, b, preferred_element_type=F32)


def _sigmoid(x):
    return 1.0 / (1.0 + jnp.exp2(x * (-LOG2E)))


def _silu(x):
    return x * _sigmoid(x)


def _split3(x):
    hi = x.astype(BF16).astype(F32)
    r = x - hi
    mid = r.astype(BF16).astype(F32)
    lo = (r - mid).astype(BF16).astype(F32)
    return hi, mid, lo


def _head_sum_matrix():
    m = np.zeros((2 * D_C, LANES), np.float32)
    for h in range(N_HEADS):
        m[h * HEAD_DIM:(h + 1) * HEAD_DIM, h] = 1.0
        m[D_C + h * HEAD_DIM:D_C + (h + 1) * HEAD_DIM, N_HEADS + h] = 1.0
    return m


def _head_bcast_matrix():
    m = np.zeros((LANES, 2 * D_C), np.float32)
    for part in range(2):
        for h in range(N_HEADS):
            m[16 * part + h, h * HEAD_DIM:(h + 1) * HEAD_DIM] = 1.0
            m[16 * part + N_HEADS + h, D_C + h * HEAD_DIM:D_C + (h + 1) * HEAD_DIM] = 1.0
    return m


def _aug_offset(h):
    return HEAD_DIM if h % 2 == 0 else 0


def _decay_place_matrix():
    m = np.zeros((LANES, 2 * N_HEADS * LANES), np.float32)
    for h in range(N_HEADS):
        qb = h * LANES + _aug_offset(h)
        kb = N_HEADS * LANES + h * LANES + _aug_offset(h)
        for part in range(3):
            m[8 * part + h, qb + part] = 1.0
            m[24, qb + 3 + part] = 1.0
            m[24, kb + part] = 1.0
            m[8 * part + h, kb + 3 + part] = -1.0
    return m


def _zero_after(token):
    bits = pltpu.bitcast(token, jnp.uint32)
    return lax.shift_right_logical(lax.shift_right_logical(bits, jnp.uint32(16)), jnp.uint32(16)).astype(F32)


def _conv_unit(base, cb, abuf_ref, convw_ref, token):
    cs = slice(cb * LANES, (cb + 1) * LANES)
    win = abuf_ref[pl.ds(base, CONV_ROWS + HALO), cs]
    win = jnp.concatenate([win[:SUBLANES] + _zero_after(token), win[SUBLANES:]], axis=0)
    conv = None
    for r in range(SUBLANES):
        part = None
        rows = CONV_ROWS + (SUBLANES if r else 0)
        for j in range(CONV_WIDTH):
            off = HALO - (CONV_WIDTH - 1) + j
            if off % SUBLANES != r:
                continue
            term = convw_ref[j:j + 1, cs] * win[off - r:off - r + rows, :]
            part = term if part is None else part + term
        part = part[r:r + CONV_ROWS, :]
        conv = part if conv is None else conv + part
    return conv


def _conv_epilogue(units, a_z, convb_ref, cng_ref, cnb_ref):
    conv = jnp.concatenate(units, axis=1) + convb_ref[...]
    mu = jnp.mean(conv, axis=-1, keepdims=True)
    xc = conv - mu
    var = jnp.mean(xc * xc, axis=-1, keepdims=True)
    ln = (xc * lax.rsqrt(var + EPS)) * cng_ref[...] + cnb_ref[...]
    return (_silu(ln) * _silu(a_z)).astype(BF16)


def _emit_interleaved(mxu_tasks, valu_tasks):
    mi = vi = 0
    mcost = vcost = 0.0
    while mi < len(mxu_tasks) or vi < len(valu_tasks):
        take_mxu = vi == len(valu_tasks) or (
            mi < len(mxu_tasks) and (mcost <= vcost or valu_tasks[vi][2] > mi))
        if take_mxu:
            cost, fn = mxu_tasks[mi]
            mi += 1
            mcost += cost
            fn()
        else:
            cost, fn, _ = valu_tasks[vi]
            vi += 1
            vcost += cost
            fn()


def _proj_kernel(x_ref, ng_ref, win_ref, bg_ref, convw_ref, convb_ref, cng_ref, cnb_ref, wa_ref,
                 gng_ref, ws_ref, bst_ref, wb_ref, qkg_ref, bf_ref, esum_ref, ebc_ref, place_ref,
                 mab_ref, gc_ref, qp_ref, kp_ref, vp_ref, scz_ref,
                 abuf_ref, ccar_ref, h_ref, pa_ref, pbig_ref, act_ref):
    i = pl.program_id(1)

    @pl.when(i == 0)
    def _():
        abuf_ref[0:HALO, :] = jnp.zeros((HALO, D_A), F32)
        ccar_ref[...] = jnp.zeros((1, LANES), F32)

    x = x_ref[0]
    ms = jnp.mean(x * x, axis=-1, keepdims=True)
    h_ref[...] = ((x * lax.rsqrt(ms + EPS)) * ng_ref[...]).astype(BF16)

    pa_ref[:, 0:3 * D_A] = _dot(h_ref[...], win_ref[:, OFF_A:OFF_A + 3 * D_A])
    pa_ref[:, 3 * D_A:] = _dot(h_ref[...], win_ref[:, OFF_F:OFF_F + LANES])
    abuf_ref[HALO:HALO + TM, :] = pa_ref[:, 0:D_A] * _sigmoid(pa_ref[:, D_A:2 * D_A])

    def mxu_tile(t):
        def fn():
            lo = t * TILE_COLS
            src = lo if lo < R_B else lo + 3 * D_A
            res = _dot(h_ref[...], win_ref[:, src:src + TILE_COLS])
            pbig_ref[:, lo:lo + TILE_COLS] = res
            tokens[t] = res[:SUBLANES, :LANES]
        return (MXU_TILE_COST, fn)

    tokens = {}
    n_tiles = W_REST // TILE_COLS
    n_units = (TM // CONV_ROWS) * (D_A // LANES)

    def conv_tasks(sub):
        rows = sub * CONV_ROWS
        units = []

        def unit(cb):
            k = sub * (D_A // LANES) + cb
            tile = min(n_tiles - 1, (k * n_tiles) // n_units)
            fn = lambda: units.append(_conv_unit(rows, cb, abuf_ref, convw_ref, tokens[tile]))
            return (CONV_UNIT_COST, fn, tile + 1)

        def epilogue():
            act_ref[rows:rows + CONV_ROWS, :] = _conv_epilogue(
                units, pa_ref[rows:rows + CONV_ROWS, 2 * D_A:3 * D_A], convb_ref, cng_ref, cnb_ref)

        return [unit(cb) for cb in range(D_A // LANES)] + [(CONV_UNIT_COST, epilogue, 0)]

    def gate_task(t):
        def fn():
            cs = slice(t * TILE_COLS, (t + 1) * TILE_COLS)
            pbig_ref[:, cs] = _sigmoid(pbig_ref[:, cs] + bg_ref[:, cs])
        return (GATE_TILE_COST, fn, t + 1)

    assert R_B % TILE_COLS == 0 and W_REST % TILE_COLS == 0
    n_gate_tiles = 3 * D_MODEL // TILE_COLS
    valu_tasks = [task for sub in range(TM // CONV_ROWS) for task in conv_tasks(sub)]
    valu_tasks += [gate_task(t) for t in range(n_gate_tiles)]
    _emit_interleaved([mxu_tile(t) for t in range(W_REST // TILE_COLS)], valu_tasks)
    abuf_ref[0:HALO, :] = abuf_ref[TM:TM + HALO, :]

    def gate(idx):
        return pbig_ref[:, R_GATE + idx * D_MODEL:R_GATE + (idx + 1) * D_MODEL]

    lane = lax.broadcasted_iota(jnp.int32, (TM, LANES), 1)

    ya = _dot(act_ref[...], wa_ref[...])

    v = pbig_ref[:, R_B + D_B:R_B + 2 * D_B]
    v = (v * lax.rsqrt(jnp.mean(v * v, axis=-1, keepdims=True) + EPS)) * gng_ref[...]
    vb = v.astype(BF16)
    n_chunks = TM // CHUNK
    row = lax.broadcasted_iota(jnp.int32, (CHUNK, CHUNK), 0)
    col = lax.broadcasted_iota(jnp.int32, (CHUNK, CHUNK), 1)
    tril = col <= row
    mixed_g = []
    for g in range(N_GROUPS_B):
        gs = slice(g * CHUNK, (g + 1) * CHUNK)
        vg = jnp.concatenate([vb[n * CHUNK:(n + 1) * CHUNK, gs] for n in range(n_chunks)], axis=1)
        wsg = jnp.where(tril, ws_ref[g], 0.0).astype(BF16)
        mixed_g.append(_dot(wsg, vg) + bst_ref[:, g:g + 1])

    qk = pbig_ref[:, R_C:R_C + 2 * D_C]
    ssq = _dot((qk * qk).astype(BF16), esum_ref[...])

    z = pa_ref[:, 3 * D_A:] + bf_ref[...]
    logf = jnp.minimum(z, 0.0) - jnp.log1p(jnp.exp(-jnp.abs(z)))
    logf = jnp.where(lane < N_HEADS, logf, 0.0)
    l_hi, l_mid, l_lo = _split3(logf)
    l_split = l_hi + pltpu.roll(l_mid, 8, 1) + pltpu.roll(l_lo, 16, 1)
    trow = lax.broadcasted_iota(jnp.int32, (TM, TM), 0)
    tcol = lax.broadcasted_iota(jnp.int32, (TM, TM), 1)
    tri = jnp.where(tcol <= trow, 1.0, 0.0).astype(BF16)
    r = _dot(tri, l_split.astype(BF16))

    mab = gate(0) * ya

    rs = lax.rsqrt(ssq * (1.0 / HEAD_DIM) + EPS)
    rs_hi = rs.astype(BF16).astype(F32)
    rs_lo = (rs - rs_hi).astype(BF16).astype(F32)
    rs_split = jnp.where(lane < 16, rs_hi, jnp.where(lane < 32, pltpu.roll(rs_lo, 16, 1), 0.0))
    rs_b = _dot(rs_split.astype(BF16), ebc_ref[...])

    c = r + pltpu.roll(r, LANES - 8, 1) + pltpu.roll(r, LANES - 16, 1)
    c = jnp.where(lane < N_HEADS, c, 0.0) + ccar_ref[...]
    ccar_ref[...] = c[TM - 1:TM, :]
    c_hi, c_mid, c_lo = _split3(c * LOG2E)
    c_split = c_hi + pltpu.roll(c_mid, 8, 1) + pltpu.roll(c_lo, 16, 1)
    c_split = jnp.where(lane == 24, 1.0, c_split)
    aug = _dot(c_split.astype(BF16), place_ref[...])

    mixed = jnp.concatenate(
        [jnp.concatenate([mixed_g[g][:, n * CHUNK:(n + 1) * CHUNK] for g in range(N_GROUPS_B)], axis=1)
         for n in range(n_chunks)], axis=0)
    u = pbig_ref[:, R_B:R_B + D_B]
    act_b = ((u * mixed) * _silu(pbig_ref[:, R_B + 2 * D_B:R_B + 3 * D_B])).astype(BF16)
    yb = _dot(act_b, wb_ref[...])

    qkn = (qk * rs_b) * qkg_ref[...]
    vv = pbig_ref[:, R_C + 2 * D_C:R_C + 3 * D_C]
    scz_ref[0] = _silu(pbig_ref[:, R_C + 3 * D_C:R_C + 4 * D_C]).astype(BF16)
    gc_ref[0] = gate(2).astype(BF16)
    for hd in range(N_HEADS):
        p = hd // 2
        own = (lane < HEAD_DIM) if hd % 2 == 0 else (lane >= HEAD_DIM)
        one_lane = HEAD_DIM if hd % 2 == 0 else 0
        q_pair = qkn[:, p * LANES:(p + 1) * LANES]
        k_pair = qkn[:, D_C + p * LANES:D_C + (p + 1) * LANES]
        v_pair = vv[:, p * LANES:(p + 1) * LANES]
        qp_ref[0, hd] = (jnp.where(own, q_pair, 0.0) + aug[:, hd * LANES:(hd + 1) * LANES]).astype(BF16)
        kp_ref[0, hd] = (jnp.where(own, k_pair, 0.0)
                         + aug[:, (N_HEADS + hd) * LANES:(N_HEADS + hd + 1) * LANES]).astype(BF16)
        vp_ref[0, hd] = jnp.where(own, v_pair, jnp.where(lane == one_lane, 1.0, 0.0)).astype(BF16)

    mab_ref[0] = (mab + gate(1) * yb).astype(BF16)


def _const_spec(shape):
    nd = len(shape)
    return pl.BlockSpec(shape, lambda b, i, _nd=nd: (0,) * _nd, pipeline_mode=pl.Buffered(1))


def _proj_call(layer, x, ng, win, bg, convw, convb, cng, cnb, wa, gng, ws, bst, wb, qkg, bf, esum, ebc, place):
    bsz, seq, _ = x.shape
    grid = (bsz, seq // TM)
    consts = (ng, win, bg, convw, convb, cng, cnb, wa, gng, ws, bst, wb, qkg, bf, esum, ebc, place)
    const_specs = [_const_spec(c.shape) for c in consts]
    const_specs[1] = pl.BlockSpec((None,) + win.shape[1:], lambda b, i: (layer, 0, 0),
                                  pipeline_mode=pl.Buffered(1))
    row_spec = lambda w: pl.BlockSpec((1, TM, w), lambda b, i: (b, i, 0))
    head_spec = pl.BlockSpec((1, N_HEADS, TM, LANES), lambda b, i: (b, 0, i, 0))
    head_shape = jax.ShapeDtypeStruct((bsz, N_HEADS, seq, LANES), BF16)
    return pl.pallas_call(
        _proj_kernel,
        grid=grid,
        in_specs=[row_spec(D_MODEL)] + const_specs,
        out_specs=[row_spec(D_MODEL), row_spec(D_MODEL), head_spec, head_spec, head_spec, row_spec(D_C)],
        out_shape=[jax.ShapeDtypeStruct((bsz, seq, D_MODEL), BF16),
                   jax.ShapeDtypeStruct((bsz, seq, D_MODEL), BF16),
                   head_shape, head_shape, head_shape,
                   jax.ShapeDtypeStruct((bsz, seq, D_C), BF16)],
        scratch_shapes=[pltpu.VMEM((TM + HALO, D_A), F32), pltpu.VMEM((1, LANES), F32),
                        pltpu.VMEM((TM, D_MODEL), BF16), pltpu.VMEM((TM, W_HEAD), F32),
                        pltpu.VMEM((TM, W_REST), F32),
                        pltpu.VMEM((TM, D_A), BF16)],
        compiler_params=pltpu.CompilerParams(
            dimension_semantics=("arbitrary", "arbitrary"), vmem_limit_bytes=VMEM_LIMIT),
        name="proj",
    )(x, *consts)


def _attn_kernel(q_ref, k_ref, v_ref, o_ref, sa_ref, sb_ref):
    i = pl.program_id(2)
    units = [(hh, st) for hh in range(2) for st in range(TQ // TQS)]
    qs = [q_ref[0, hh, st * TQS:(st + 1) * TQS, :] for hh, st in units]

    per_tile = TQ // TK
    all_units = tuple(range(len(units)))
    seeing = [tuple(u for u in all_units if (units[u][1] + 1) * TQS > d * TK) for d in range(per_tile)]
    bufs = (sa_ref, sb_ref)

    def key_rows(blk):
        return slice(blk * TK, (blk + 1) * TK)

    def qk(blk, s_ref, active):
        for u in active:
            s_ref[u] = lax.dot_general(k_ref[0, units[u][0], key_rows(blk), :], qs[u],
                                       (((1,), (1,)), ((), ())), preferred_element_type=F32)

    def softmax_pv(blk, s_ref, carry, active, diag):
        def scores(u):
            s = s_ref[u]
            if diag is not None:
                kpos = lax.broadcasted_iota(jnp.int32, (TK, TQS), 0) + diag * TK
                qpos = lax.broadcasted_iota(jnp.int32, (TK, TQS), 1) + units[u][1] * TQS
                s = jnp.where(kpos <= qpos, s, MASK_VALUE)
            return s

        m_new = {u: jnp.maximum(carry[u][0], jnp.max(scores(u), axis=0, keepdims=True)) for u in active}
        probs = {u: jnp.exp2(scores(u) - m_new[u]).astype(BF16) for u in active}
        out = list(carry)
        for u in active:
            pv = lax.dot_general(v_ref[0, units[u][0], key_rows(blk), :], probs[u],
                                 (((0,), (0,)), ((), ())), preferred_element_type=F32)
            out[u] = (m_new[u], jnp.exp2(carry[u][0] - m_new[u]) * carry[u][1] + pv)
        return tuple(out)

    def tile(n_full):
        def users(blk):
            return all_units if blk < n_full else seeing[blk - n_full]

        n_blocks = n_full + per_tile
        carry = tuple((jnp.full((1, TQS), MASK_VALUE, F32), jnp.zeros((LANES, TQS), F32)) for _ in units)
        qk(0, bufs[0], users(0))
        for blk in range(n_blocks):
            if blk + 1 < n_blocks:
                qk(blk + 1, bufs[(blk + 1) % 2], users(blk + 1))
            carry = softmax_pv(blk, bufs[blk % 2], carry, users(blk), blk - n_full if blk >= n_full else None)

        n_st = TQ // TQS
        acc0 = jnp.concatenate([carry[st][1] for st in range(n_st)], axis=1)
        acc1 = jnp.concatenate([carry[n_st + st][1] for st in range(n_st)], axis=1)
        feat = lax.broadcasted_iota(jnp.int32, (LANES, TQ), 0)
        o_t = jnp.where(feat < HEAD_DIM, acc0 / acc0[HEAD_DIM:HEAD_DIM + 1, :], acc1 / acc1[0:1, :])
        o_ref[0] = o_t.T.astype(BF16)

    for case in range(k_ref.shape[2] // TQ):
        pl.when(i == case)(functools.partial(tile, case * per_tile))


def _attn_call(qp, kp, vp):
    bsz, _, seq, _ = qp.shape
    grid = (bsz, N_HEADS // 2, seq // TQ)
    kv_spec = pl.BlockSpec((1, 2, seq, LANES), lambda b, p, i: (b, p, 0, 0))
    return pl.pallas_call(
        _attn_kernel,
        grid=grid,
        in_specs=[pl.BlockSpec((1, 2, TQ, LANES), lambda b, p, i: (b, p, i, 0)), kv_spec, kv_spec],
        out_specs=pl.BlockSpec((1, TQ, LANES), lambda b, p, i: (b, i, p)),
        out_shape=jax.ShapeDtypeStruct((bsz, seq, D_C), BF16),
        scratch_shapes=[pltpu.VMEM((2 * (TQ // TQS), TK, TQS), F32) for _ in range(2)],
        compiler_params=pltpu.CompilerParams(
            dimension_semantics=("arbitrary", "arbitrary", "arbitrary"), vmem_limit_bytes=VMEM_LIMIT),
        name="attn",
    )(qp, kp, vp)


def _merge_kernel(x_ref, mab_ref, gc_ref, o_ref, scz_ref, wc_ref, wout_ref, out_ref):
    act_c = (o_ref[0].astype(F32) * scz_ref[0].astype(F32)).astype(BF16)
    merged = mab_ref[0].astype(F32) + gc_ref[0].astype(F32) * _dot(act_c, wc_ref[...])
    out_ref[0] = x_ref[0] + _dot(merged.astype(BF16), wout_ref[...])


def _merge_call(x, mab, gc, o, scz, wc, wout):
    bsz, seq, _ = x.shape
    row_spec = lambda w: pl.BlockSpec((1, TM_MERGE, w), lambda b, i: (b, i, 0))
    return pl.pallas_call(
        _merge_kernel,
        grid=(bsz, seq // TM_MERGE),
        in_specs=[row_spec(D_MODEL), row_spec(D_MODEL), row_spec(D_MODEL), row_spec(D_C), row_spec(D_C),
                  _const_spec(wc.shape), _const_spec(wout.shape)],
        out_specs=row_spec(D_MODEL),
        out_shape=jax.ShapeDtypeStruct(x.shape, F32),
        compiler_params=pltpu.CompilerParams(
            dimension_semantics=("arbitrary", "arbitrary"), vmem_limit_bytes=VMEM_LIMIT),
        name="merge",
    )(x, mab, gc, o, scz, wc, wout)


def _cast_kernel(w_ref, o_ref):
    col = lax.broadcasted_iota(jnp.int32, w_ref.shape, 2) + pl.program_id(1) * CAST_COLS
    o_ref[...] = jnp.where(col < N_IN, w_ref[...], 0.0).astype(BF16)


def _cast_w_in(w_in):
    depth = w_in.shape[0]
    spec = pl.BlockSpec((1, D_MODEL, CAST_COLS), lambda l, j: (l, 0, j))
    return pl.pallas_call(
        _cast_kernel,
        grid=(depth, pl.cdiv(N_IN_PAD, CAST_COLS)),
        in_specs=[spec],
        out_specs=spec,
        out_shape=jax.ShapeDtypeStruct((depth, D_MODEL, N_IN_PAD), BF16),
        name="cast_w_in",
    )(w_in)


def kernel(x, norm_g, w_in, b_gate, conv_w, conv_b, conv_norm_g, conv_norm_b, w_a, gmlp_norm_g,
           w_s, b_s, w_b, q_norm_g, k_norm_g, b_f, w_c, w_out):
    depth = w_in.shape[0]
    esum = jnp.asarray(_head_sum_matrix(), BF16)
    ebc = jnp.asarray(_head_bcast_matrix(), BF16)
    place = jnp.asarray(_decay_place_matrix(), BF16)
    win = _cast_w_in(w_in)
    qkg = jnp.concatenate([jnp.tile(q_norm_g, (1, N_HEADS)) * QK_SCALE, jnp.tile(k_norm_g, (1, N_HEADS))], axis=1)
    bf = jnp.pad(b_f, ((0, 0), (0, LANES - N_HEADS)))
    row = lambda a, l: a[l][None, :]
    for l in range(depth):
        mab, gc, qp, kp, vp, scz = _proj_call(
            l, x, row(norm_g, l), win, row(b_gate, l), conv_w[l], row(conv_b, l), row(conv_norm_g, l),
            row(conv_norm_b, l), w_a[l].astype(BF16), row(gmlp_norm_g, l), w_s[l], b_s[l].T,
            w_b[l].astype(BF16), row(qkg, l), row(bf, l), esum, ebc, place)
        o = _attn_call(qp, kp, vp)
        x = _merge_call(x, mab, gc, o, scz, w_c[l].astype(BF16), w_out[l].astype(BF16))
    return x
```

```python
import functools
import math

import numpy as np
import jax
import jax.numpy as jnp
from jax import lax
from jax.experimental import pallas as pl
from jax.experimental.pallas import tpu as pltpu

D_MODEL = 1024
D_A = 512
D_B = 512
N_HEADS = 8
HEAD_DIM = 64
D_C = N_HEADS * HEAD_DIM
CONV_WIDTH = 31
CHUNK = 128
N_GROUPS_B = 4
EPS = 1e-6
N_IN = 3 * D_MODEL + 3 * D_A + 3 * D_B + 4 * D_C + N_HEADS

LANES = 128
SUBLANES = 8
MXU_COLS = 256
N_IN_PAD = ((N_IN + LANES - 1) // LANES) * LANES
TILE_COLS = MXU_COLS
OFF_GATE = 0
OFF_A = 3 * D_MODEL
OFF_B = OFF_A + 3 * D_A
OFF_C = OFF_B + 3 * D_B
OFF_F = OFF_C + 4 * D_C
W_HEAD = 3 * D_A + LANES
W_REST = 3 * D_MODEL + 3 * D_B + 4 * D_C
R_GATE = 0
R_B = R_GATE + 3 * D_MODEL
R_C = R_B + 3 * D_B

HALO = 32
CONV_ROWS = 64
TM = 256
TM_MERGE = 1024
TQ = 1024
TK = 256
TQS = 256
MXU_TILE_COST = 130
CONV_UNIT_COST = 235
GATE_TILE_COST = 100
LOG2E = math.log2(math.e)
QK_SCALE = LOG2E / math.sqrt(HEAD_DIM)
MASK_VALUE = -1e30
VMEM_LIMIT = 58 * 1024 * 1024

F32 = jnp.float32
BF16 = jnp.bfloat16


def _dot(a, b):
    return jnp.dot(a, b, preferred_element_type=F32)


def _sigmoid(x):
    return 1.0 / (1.0 + jnp.exp2(x * (-LOG2E)))


def _silu(x):
    return x * _sigmoid(x)


def _split3(x):
    hi = x.astype(BF16).astype(F32)
    r = x - hi
    mid = r.astype(BF16).astype(F32)
    lo = (r - mid).astype(BF16).astype(F32)
    return hi, mid, lo


def _head_sum_matrix():
    m = np.zeros((2 * D_C, LANES), np.float32)
    for h in range(N_HEADS):
        m[h * HEAD_DIM:(h + 1) * HEAD_DIM, h] = 1.0
        m[D_C + h * HEAD_DIM:D_C + (h + 1) * HEAD_DIM, N_HEADS + h] = 1.0
    return m


def _head_bcast_matrix():
    m = np.zeros((LANES, 2 * D_C), np.float32)
    for part in range(2):
        for h in range(N_HEADS):
            m[16 * part + h, h * HEAD_DIM:(h + 1) * HEAD_DIM] = 1.0
            m[16 * part + N_HEADS + h, D_C + h * HEAD_DIM:D_C + (h + 1) * HEAD_DIM] = 1.0
    return m


def _aug_offset(h):
    return HEAD_DIM if h % 2 == 0 else 0


def _decay_place_matrix():
    m = np.zeros((LANES, 2 * N_HEADS * LANES), np.float32)
    for h in range(N_HEADS):
        qb = h * LANES + _aug_offset(h)
        kb = N_HEADS * LANES + h * LANES + _aug_offset(h)
        for part in range(3):
            m[8 * part + h, qb + part] = 1.0
            m[24, qb + 3 + part] = 1.0
            m[24, kb + part] = 1.0
            m[8 * part + h, kb + 3 + part] = -1.0
    return m


def _zero_after(token):
    bits = pltpu.bitcast(token, jnp.uint32)
    return lax.shift_right_logical(lax.shift_right_logical(bits, jnp.uint32(16)), jnp.uint32(16)).astype(F32)


def _conv_unit(base, cb, abuf_ref, convw_ref, token):
    cs = slice(cb * LANES, (cb + 1) * LANES)
    win = abuf_ref[pl.ds(base, CONV_ROWS + HALO), cs]
    win = jnp.concatenate([win[:SUBLANES] + _zero_after(token), win[SUBLANES:]], axis=0)
    conv = None
    for r in range(SUBLANES):
        part = None
        rows = CONV_ROWS + (SUBLANES if r else 0)
        for j in range(CONV_WIDTH):
            off = HALO - (CONV_WIDTH - 1) + j
            if off % SUBLANES != r:
                continue
            term = convw_ref[j:j + 1, cs] * win[off - r:off - r + rows, :]
            part = term if part is None else part + term
        part = part[r:r + CONV_ROWS, :]
        conv = part if conv is None else conv + part
    return conv


def _conv_epilogue(units, a_z, convb_ref, cng_ref, cnb_ref):
    conv = jnp.concatenate(units, axis=1) + convb_ref[...]
    mu = jnp.mean(conv, axis=-1, keepdims=True)
    xc = conv - mu
    var = jnp.mean(xc * xc, axis=-1, keepdims=True)
    ln = (xc * lax.rsqrt(var + EPS)) * cng_ref[...] + cnb_ref[...]
    return (_silu(ln) * _silu(a_z)).astype(BF16)


def _emit_interleaved(mxu_tasks, valu_tasks):
    mi = vi = 0
    mcost = vcost = 0.0
    while mi < len(mxu_tasks) or vi < len(valu_tasks):
        take_mxu = vi == len(valu_tasks) or (
            mi < len(mxu_tasks) and (mcost <= vcost or valu_tasks[vi][2] > mi))
        if take_mxu:
            cost, fn = mxu_tasks[mi]
            mi += 1
            mcost += cost
            fn()
        else:
            cost, fn, _ = valu_tasks[vi]
            vi += 1
            vcost += cost
            fn()


def _proj_kernel(x_ref, ng_ref, win_ref, bg_ref, convw_ref, convb_ref, cng_ref, cnb_ref, wa_ref,
                 gng_ref, ws_ref, bst_ref, wb_ref, qkg_ref, bf_ref, esum_ref, ebc_ref, place_ref,
                 mab_ref, gc_ref, qp_ref, kp_ref, vp_ref, scz_ref,
                 abuf_ref, ccar_ref, h_ref, pa_ref, pbig_ref, act_ref):
    i = pl.program_id(1)

    @pl.when(i == 0)
    def _():
        abuf_ref[0:HALO, :] = jnp.zeros((HALO, D_A), F32)
        ccar_ref[...] = jnp.zeros((1, LANES), F32)

    x = x_ref[0]
    ms = jnp.mean(x * x, axis=-1, keepdims=True)
    h_ref[...] = ((x * lax.rsqrt(ms + EPS)) * ng_ref[...]).astype(BF16)

    pa_ref[:, 0:3 * D_A] = _dot(h_ref[...], win_ref[:, OFF_A:OFF_A + 3 * D_A])
    pa_ref[:, 3 * D_A:] = _dot(h_ref[...], win_ref[:, OFF_F:OFF_F + LANES])
    abuf_ref[HALO:HALO + TM, :] = pa_ref[:, 0:D_A] * _sigmoid(pa_ref[:, D_A:2 * D_A])

    def mxu_tile(t):
        def fn():
            lo = t * TILE_COLS
            src = lo if lo < R_B else lo + 3 * D_A
            res = _dot(h_ref[...], win_ref[:, src:src + TILE_COLS])
            pbig_ref[:, lo:lo + TILE_COLS] = res
            tokens[t] = res[:SUBLANES, :LANES]
        return (MXU_TILE_COST, fn)

    tokens = {}
    n_tiles = W_REST // TILE_COLS
    n_units = (TM // CONV_ROWS) * (D_A // LANES)

    def conv_tasks(sub):
        rows = sub * CONV_ROWS
        units = []

        def unit(cb):
            k = sub * (D_A // LANES) + cb
            tile = min(n_tiles - 1, (k * n_tiles) // n_units)
            fn = lambda: units.append(_conv_unit(rows, cb, abuf_ref, convw_ref, tokens[tile]))
            return (CONV_UNIT_COST, fn, tile + 1)

        def epilogue():
            act_ref[rows:rows + CONV_ROWS, :] = _conv_epilogue(
                units, pa_ref[rows:rows + CONV_ROWS, 2 * D_A:3 * D_A], convb_ref, cng_ref, cnb_ref)

        return [unit(cb) for cb in range(D_A // LANES)] + [(CONV_UNIT_COST, epilogue, 0)]

    def gate_task(t):
        def fn():
            cs = slice(t * TILE_COLS, (t + 1) * TILE_COLS)
            pbig_ref[:, cs] = _sigmoid(pbig_ref[:, cs] + bg_ref[:, cs])
        return (GATE_TILE_COST, fn, t + 1)

    assert R_B % TILE_COLS == 0 and W_REST % TILE_COLS == 0
    n_gate_tiles = 3 * D_MODEL // TILE_COLS
    valu_tasks = [task for sub in range(TM // CONV_ROWS) for task in conv_tasks(sub)]
    valu_tasks += [gate_task(t) for t in range(n_gate_tiles)]
    _emit_interleaved([mxu_tile(t) for t in range(W_REST // TILE_COLS)], valu_tasks)
    abuf_ref[0:HALO, :] = abuf_ref[TM:TM + HALO, :]

    def gate(idx):
        return pbig_ref[:, R_GATE + idx * D_MODEL:R_GATE + (idx + 1) * D_MODEL]

    lane = lax.broadcasted_iota(jnp.int32, (TM, LANES), 1)

    ya = _dot(act_ref[...], wa_ref[...])

    v = pbig_ref[:, R_B + D_B:R_B + 2 * D_B]
    v = (v * lax.rsqrt(jnp.mean(v * v, axis=-1, keepdims=True) + EPS)) * gng_ref[...]
    vb = v.astype(BF16)
    n_chunks = TM // CHUNK
    row = lax.broadcasted_iota(jnp.int32, (CHUNK, CHUNK), 0)
    col = lax.broadcasted_iota(jnp.int32, (CHUNK, CHUNK), 1)
    tril = col <= row
    mixed_g = []
    for g in range(N_GROUPS_B):
        gs = slice(g * CHUNK, (g + 1) * CHUNK)
        vg = jnp.concatenate([vb[n * CHUNK:(n + 1) * CHUNK, gs] for n in range(n_chunks)], axis=1)
        wsg = jnp.where(tril, ws_ref[g], 0.0).astype(BF16)
        mixed_g.append(_dot(wsg, vg) + bst_ref[:, g:g + 1])

    qk = pbig_ref[:, R_C:R_C + 2 * D_C]
    ssq = _dot((qk * qk).astype(BF16), esum_ref[...])

    z = pa_ref[:, 3 * D_A:] + bf_ref[...]
    logf = jnp.minimum(z, 0.0) - jnp.log1p(jnp.exp(-jnp.abs(z)))
    logf = jnp.where(lane < N_HEADS, logf, 0.0)
    l_hi, l_mid, l_lo = _split3(logf)
    l_split = l_hi + pltpu.roll(l_mid, 8, 1) + pltpu.roll(l_lo, 16, 1)
    trow = lax.broadcasted_iota(jnp.int32, (TM, TM), 0)
    tcol = lax.broadcasted_iota(jnp.int32, (TM, TM), 1)
    tri = jnp.where(tcol <= trow, 1.0, 0.0).astype(BF16)
    r = _dot(tri, l_split.astype(BF16))

    mab = gate(0) * ya

    rs = lax.rsqrt(ssq * (1.0 / HEAD_DIM) + EPS)
    rs_hi = rs.astype(BF16).astype(F32)
    rs_lo = (rs - rs_hi).astype(BF16).astype(F32)
    rs_split = jnp.where(lane < 16, rs_hi, jnp.where(lane < 32, pltpu.roll(rs_lo, 16, 1), 0.0))
    rs_b = _dot(rs_split.astype(BF16), ebc_ref[...])

    c = r + pltpu.roll(r, LANES - 8, 1) + pltpu.roll(r, LANES - 16, 1)
    c = jnp.where(lane < N_HEADS, c, 0.0) + ccar_ref[...]
    ccar_ref[...] = c[TM - 1:TM, :]
    c_hi, c_mid, c_lo = _split3(c * LOG2E)
    c_split = c_hi + pltpu.roll(c_mid, 8, 1) + pltpu.roll(c_lo, 16, 1)
    c_split = jnp.where(lane == 24, 1.0, c_split)
    aug = _dot(c_split.astype(BF16), place_ref[...])

    mixed = jnp.concatenate(
        [jnp.concatenate([mixed_g[g][:, n * CHUNK:(n + 1) * CHUNK] for g in range(N_GROUPS_B)], axis=1)
         for n in range(n_chunks)], axis=0)
    u = pbig_ref[:, R_B:R_B + D_B]
    act_b = ((u * mixed) * _silu(pbig_ref[:, R_B + 2 * D_B:R_B + 3 * D_B])).astype(BF16)
    yb = _dot(act_b, wb_ref[...])

    qkn = (qk * rs_b) * qkg_ref[...]
    vv = pbig_ref[:, R_C + 2 * D_C:R_C + 3 * D_C]
    scz_ref[0] = _silu(pbig_ref[:, R_C + 3 * D_C:R_C + 4 * D_C]).astype(BF16)
    gc_ref[0] = gate(2).astype(BF16)
    for hd in range(N_HEADS):
        p = hd // 2
        own = (lane < HEAD_DIM) if hd % 2 == 0 else (lane >= HEAD_DIM)
        one_lane = HEAD_DIM if hd % 2 == 0 else 0
        q_pair = qkn[:, p * LANES:(p + 1) * LANES]
        k_pair = qkn[:, D_C + p * LANES:D_C + (p + 1) * LANES]
        v_pair = vv[:, p * LANES:(p + 1) * LANES]
        qp_ref[0, hd] = (jnp.where(own, q_pair, 0.0) + aug[:, hd * LANES:(hd + 1) * LANES]).astype(BF16)
        kp_ref[0, hd] = (jnp.where(own, k_pair, 0.0)
                         + aug[:, (N_HEADS + hd) * LANES:(N_HEADS + hd + 1) * LANES]).astype(BF16)
        vp_ref[0, hd] = jnp.where(own, v_pair, jnp.where(lane == one_lane, 1.0, 0.0)).astype(BF16)

    mab_ref[0] = (mab + gate(1) * yb).astype(BF16)


def _const_spec(shape):
    nd = len(shape)
    return pl.BlockSpec(shape, lambda b, i, _nd=nd: (0,) * _nd, pipeline_mode=pl.Buffered(1))


def _proj_call(layer, x, ng, win, bg, convw, convb, cng, cnb, wa, gng, ws, bst, wb, qkg, bf, esum, ebc, place):
    bsz, seq, _ = x.shape
    grid = (bsz, seq // TM)
    consts = (ng, win, bg, convw, convb, cng, cnb, wa, gng, ws, bst, wb, qkg, bf, esum, ebc, place)
    const_specs = [_const_spec(c.shape) for c in consts]
    const_specs[1] = pl.BlockSpec((None,) + win.shape[1:], lambda b, i: (layer, 0, 0),
                                  pipeline_mode=pl.Buffered(1))
    row_spec = lambda w: pl.BlockSpec((1, TM, w), lambda b, i: (b, i, 0))
    head_spec = pl.BlockSpec((1, N_HEADS, TM, LANES), lambda b, i: (b, 0, i, 0))
    head_shape = jax.ShapeDtypeStruct((bsz, N_HEADS, seq, LANES), BF16)
    return pl.pallas_call(
        _proj_kernel,
        grid=grid,
        in_specs=[row_spec(D_MODEL)] + const_specs,
        out_specs=[row_spec(D_MODEL), row_spec(D_MODEL), head_spec, head_spec, head_spec, row_spec(D_C)],
        out_shape=[jax.ShapeDtypeStruct((bsz, seq, D_MODEL), BF16),
                   jax.ShapeDtypeStruct((bsz, seq, D_MODEL), BF16),
                   head_shape, head_shape, head_shape,
                   jax.ShapeDtypeStruct((bsz, seq, D_C), BF16)],
        scratch_shapes=[pltpu.VMEM((TM + HALO, D_A), F32), pltpu.VMEM((1, LANES), F32),
                        pltpu.VMEM((TM, D_MODEL), BF16), pltpu.VMEM((TM, W_HEAD), F32),
                        pltpu.VMEM((TM, W_REST), F32),
                        pltpu.VMEM((TM, D_A), BF16)],
        compiler_params=pltpu.CompilerParams(
            dimension_semantics=("arbitrary", "arbitrary"), vmem_limit_bytes=VMEM_LIMIT),
        name="proj",
    )(x, *consts)


def _attn_kernel(q_ref, k_ref, v_ref, o_ref, sa_ref, sb_ref):
    i = pl.program_id(2)
    units = [(hh, st) for hh in range(2) for st in range(TQ // TQS)]
    qs = [q_ref[0, hh, st * TQS:(st + 1) * TQS, :] for hh, st in units]

    per_tile = TQ // TK
    all_units = tuple(range(len(units)))
    seeing = [tuple(u for u in all_units if (units[u][1] + 1) * TQS > d * TK) for d in range(per_tile)]
    bufs = (sa_ref, sb_ref)

    def key_rows(blk):
        return slice(blk * TK, (blk + 1) * TK)

    def qk(blk, s_ref, active):
        for u in active:
            s_ref[u] = lax.dot_general(k_ref[0, units[u][0], key_rows(blk), :], qs[u],
                                       (((1,), (1,)), ((), ())), preferred_element_type=F32)

    def softmax_pv(blk, s_ref, carry, active, diag):
        def scores(u):
            s = s_ref[u]
            if diag is not None:
                kpos = lax.broadcasted_iota(jnp.int32, (TK, TQS), 0) + diag * TK
                qpos = lax.broadcasted_iota(jnp.int32, (TK, TQS), 1) + units[u][1] * TQS
                s = jnp.where(kpos <= qpos, s, MASK_VALUE)
            return s

        m_new = {u: jnp.maximum(carry[u][0], jnp.max(scores(u), axis=0, keepdims=True)) for u in active}
        probs = {u: jnp.exp2(scores(u) - m_new[u]).astype(BF16) for u in active}
        out = list(carry)
        for u in active:
            pv = lax.dot_general(v_ref[0, units[u][0], key_rows(blk), :], probs[u],
                                 (((0,), (0,)), ((), ())), preferred_element_type=F32)
            out[u] = (m_new[u], jnp.exp2(carry[u][0] - m_new[u]) * carry[u][1] + pv)
        return tuple(out)

    def tile(n_full):
        def users(blk):
            return all_units if blk < n_full else seeing[blk - n_full]

        n_blocks = n_full + per_tile
        carry = tuple((jnp.full((1, TQS), MASK_VALUE, F32), jnp.zeros((LANES, TQS), F32)) for _ in units)
        qk(0, bufs[0], users(0))
        for blk in range(n_blocks):
            if blk + 1 < n_blocks:
                qk(blk + 1, bufs[(blk + 1) % 2], users(blk + 1))
            carry = softmax_pv(blk, bufs[blk % 2], carry, users(blk), blk - n_full if blk >= n_full else None)

        n_st = TQ // TQS
        acc0 = jnp.concatenate([carry[st][1] for st in range(n_st)], axis=1)
        acc1 = jnp.concatenate([carry[n_st + st][1] for st in range(n_st)], axis=1)
        feat = lax.broadcasted_iota(jnp.int32, (LANES, TQ), 0)
        o_t = jnp.where(feat < HEAD_DIM, acc0 / acc0[HEAD_DIM:HEAD_DIM + 1, :], acc1 / acc1[0:1, :])
        o_ref[0] = o_t.T.astype(BF16)

    for case in range(k_ref.shape[2] // TQ):
        pl.when(i == case)(functools.partial(tile, case * per_tile))


def _attn_call(qp, kp, vp):
    bsz, _, seq, _ = qp.shape
    grid = (bsz, N_HEADS // 2, seq // TQ)
    kv_spec = pl.BlockSpec((1, 2, seq, LANES), lambda b, p, i: (b, p, 0, 0))
    return pl.pallas_call(
        _attn_kernel,
        grid=grid,
        in_specs=[pl.BlockSpec((1, 2, TQ, LANES), lambda b, p, i: (b, p, i, 0)), kv_spec, kv_spec],
        out_specs=pl.BlockSpec((1, TQ, LANES), lambda b, p, i: (b, i, p)),
        out_shape=jax.ShapeDtypeStruct((bsz, seq, D_C), BF16),
        scratch_shapes=[pltpu.VMEM((2 * (TQ // TQS), TK, TQS), F32) for _ in range(2)],
        compiler_params=pltpu.CompilerParams(
            dimension_semantics=("arbitrary", "arbitrary", "arbitrary"), vmem_limit_bytes=VMEM_LIMIT),
        name="attn",
    )(qp, kp, vp)


def _merge_kernel(x_ref, mab_ref, gc_ref, o_ref, scz_ref, wc_ref, wout_ref, out_ref):
    act_c = (o_ref[0].astype(F32) * scz_ref[0].astype(F32)).astype(BF16)
    merged = mab_ref[0].astype(F32) + gc_ref[0].astype(F32) * _dot(act_c, wc_ref[...])
    out_ref[0] = x_ref[0] + _dot(merged.astype(BF16), wout_ref[...])


def _merge_call(x, mab, gc, o, scz, wc, wout):
    bsz, seq, _ = x.shape
    row_spec = lambda w: pl.BlockSpec((1, TM_MERGE, w), lambda b, i: (b, i, 0))
    return pl.pallas_call(
        _merge_kernel,
        grid=(bsz, seq // TM_MERGE),
        in_specs=[row_spec(D_MODEL), row_spec(D_MODEL), row_spec(D_MODEL), row_spec(D_C), row_spec(D_C),
                  _const_spec(wc.shape), _const_spec(wout.shape)],
        out_specs=row_spec(D_MODEL),
        out_shape=jax.ShapeDtypeStruct(x.shape, F32),
        compiler_params=pltpu.CompilerParams(
            dimension_semantics=("arbitrary", "arbitrary"), vmem_limit_bytes=VMEM_LIMIT),
        name="merge",
    )(x, mab, gc, o, scz, wc, wout)


def kernel(x, norm_g, w_in, b_gate, conv_w, conv_b, conv_norm_g, conv_norm_b, w_a, gmlp_norm_g,
           w_s, b_s, w_b, q_norm_g, k_norm_g, b_f, w_c, w_out):
    depth = w_in.shape[0]
    esum = jnp.asarray(_head_sum_matrix(), BF16)
    ebc = jnp.asarray(_head_bcast_matrix(), BF16)
    place = jnp.asarray(_decay_place_matrix(), BF16)
    win = jnp.pad(w_in.astype(BF16), ((0, 0), (0, 0), (0, N_IN_PAD - N_IN)))
    qkg = jnp.concatenate([jnp.tile(q_norm_g, (1, N_HEADS)) * QK_SCALE, jnp.tile(k_norm_g, (1, N_HEADS))], axis=1)
    bf = jnp.pad(b_f, ((0, 0), (0, LANES - N_HEADS)))
    row = lambda a, l: a[l][None, :]
    for l in range(depth):
        mab, gc, qp, kp, vp, scz = _proj_call(
            l, x, row(norm_g, l), win, row(b_gate, l), conv_w[l], row(conv_b, l), row(conv_norm_g, l),
            row(conv_norm_b, l), w_a[l].astype(BF16), row(gmlp_norm_g, l), w_s[l], b_s[l].T,
            w_b[l].astype(BF16), row(qkg, l), row(bf, l), esum, ebc, place)
        o = _attn_call(qp, kp, vp)
        x = _merge_call(x, mab, gc, o, scz, w_c[l].astype(BF16), w_out[l].astype(BF16))
    return x
```

```python
import functools
import math

import numpy as np
import jax
import jax.numpy as jnp
from jax import lax
from jax.experimental import pallas as pl
from jax.experimental.pallas import tpu as pltpu

D_MODEL = 1024
D_A = 512
D_B = 512
N_HEADS = 8
HEAD_DIM = 64
D_C = N_HEADS * HEAD_DIM
CONV_WIDTH = 31
CHUNK = 128
N_GROUPS_B = 4
EPS = 1e-6
N_IN = 3 * D_MODEL + 3 * D_A + 3 * D_B + 4 * D_C + N_HEADS

LANES = 128
SUBLANES = 8
MXU_COLS = 256
N_IN_PAD = ((N_IN + LANES - 1) // LANES) * LANES
TILE_COLS = MXU_COLS
OFF_GATE = 0
OFF_A = 3 * D_MODEL
OFF_B = OFF_A + 3 * D_A
OFF_C = OFF_B + 3 * D_B
OFF_F = OFF_C + 4 * D_C
W_HEAD = 3 * D_A + LANES
W_REST = 3 * D_MODEL + 3 * D_B + 4 * D_C
R_GATE = 0
R_B = R_GATE + 3 * D_MODEL
R_C = R_B + 3 * D_B

HALO = 32
CONV_ROWS = 64
TM = 256
ROW_GROUPS = 2
TM_MERGE = 1024
TQ = 1024
TK = 256
TQS = 256
MXU_TILE_COST = 130
CONV_UNIT_COST = 235
GATE_TILE_COST = 100
LOG2E = math.log2(math.e)
QK_SCALE = LOG2E / math.sqrt(HEAD_DIM)
MASK_VALUE = -1e30
VMEM_LIMIT = 58 * 1024 * 1024

F32 = jnp.float32
BF16 = jnp.bfloat16


def _dot(a, b):
    return jnp.dot(a, b, preferred_element_type=F32)


def _sigmoid(x):
    return 1.0 / (1.0 + jnp.exp2(x * (-LOG2E)))


def _silu(x):
    return x * _sigmoid(x)


def _split3(x):
    hi = x.astype(BF16).astype(F32)
    r = x - hi
    mid = r.astype(BF16).astype(F32)
    lo = (r - mid).astype(BF16).astype(F32)
    return hi, mid, lo


def _head_sum_matrix():
    m = np.zeros((2 * D_C, LANES), np.float32)
    for h in range(N_HEADS):
        m[h * HEAD_DIM:(h + 1) * HEAD_DIM, h] = 1.0
        m[D_C + h * HEAD_DIM:D_C + (h + 1) * HEAD_DIM, N_HEADS + h] = 1.0
    return m


def _head_bcast_matrix():
    m = np.zeros((LANES, 2 * D_C), np.float32)
    for part in range(2):
        for h in range(N_HEADS):
            m[16 * part + h, h * HEAD_DIM:(h + 1) * HEAD_DIM] = 1.0
            m[16 * part + N_HEADS + h, D_C + h * HEAD_DIM:D_C + (h + 1) * HEAD_DIM] = 1.0
    return m


def _aug_offset(h):
    return HEAD_DIM if h % 2 == 0 else 0


def _decay_place_matrix():
    m = np.zeros((LANES, 2 * N_HEADS * LANES), np.float32)
    for h in range(N_HEADS):
        qb = h * LANES + _aug_offset(h)
        kb = N_HEADS * LANES + h * LANES + _aug_offset(h)
        for part in range(3):
            m[8 * part + h, qb + part] = 1.0
            m[24, qb + 3 + part] = 1.0
            m[24, kb + part] = 1.0
            m[8 * part + h, kb + 3 + part] = -1.0
    return m


def _zero_after(token):
    bits = pltpu.bitcast(token, jnp.uint32)
    return lax.shift_right_logical(lax.shift_right_logical(bits, jnp.uint32(16)), jnp.uint32(16)).astype(F32)


def _conv_unit(base, cb, abuf_ref, convw_ref, token):
    cs = slice(cb * LANES, (cb + 1) * LANES)
    win = abuf_ref[pl.ds(base, CONV_ROWS + HALO), cs]
    win = jnp.concatenate([win[:SUBLANES] + _zero_after(token), win[SUBLANES:]], axis=0)
    conv = None
    for r in range(SUBLANES):
        part = None
        rows = CONV_ROWS + (SUBLANES if r else 0)
        for j in range(CONV_WIDTH):
            off = HALO - (CONV_WIDTH - 1) + j
            if off % SUBLANES != r:
                continue
            term = convw_ref[j:j + 1, cs] * win[off - r:off - r + rows, :]
            part = term if part is None else part + term
        part = part[r:r + CONV_ROWS, :]
        conv = part if conv is None else conv + part
    return conv


def _conv_epilogue(units, a_z, convb_ref, cng_ref, cnb_ref):
    conv = jnp.concatenate(units, axis=1) + convb_ref[...]
    mu = jnp.mean(conv, axis=-1, keepdims=True)
    xc = conv - mu
    var = jnp.mean(xc * xc, axis=-1, keepdims=True)
    ln = (xc * lax.rsqrt(var + EPS)) * cng_ref[...] + cnb_ref[...]
    return (_silu(ln) * _silu(a_z)).astype(BF16)


def _emit_interleaved(mxu_tasks, valu_tasks):
    mi = vi = 0
    mcost = vcost = 0.0
    while mi < len(mxu_tasks) or vi < len(valu_tasks):
        take_mxu = vi == len(valu_tasks) or (
            mi < len(mxu_tasks) and (mcost <= vcost or valu_tasks[vi][2] > mi))
        if take_mxu:
            cost, fn = mxu_tasks[mi]
            mi += 1
            mcost += cost
            fn()
        else:
            cost, fn, _ = valu_tasks[vi]
            vi += 1
            vcost += cost
            fn()


def _proj_rows(r0, x_ref, ng_ref, win_ref, bg_ref, convw_ref, convb_ref, cng_ref, cnb_ref, wa_ref,
               gng_ref, ws_ref, bst_ref, wb_ref, qkg_ref, bf_ref, esum_ref, ebc_ref, place_ref,
               mab_ref, gc_ref, qp_ref, kp_ref, vp_ref, scz_ref,
               abuf_ref, ccar_ref, h_ref, pa_ref, pbig_ref, act_ref):
    out_rows = slice(r0, r0 + TM)
    x = x_ref[0, out_rows]
    ms = jnp.mean(x * x, axis=-1, keepdims=True)
    h_ref[...] = ((x * lax.rsqrt(ms + EPS)) * ng_ref[...]).astype(BF16)

    pa_ref[:, 0:3 * D_A] = _dot(h_ref[...], win_ref[:, OFF_A:OFF_A + 3 * D_A])
    pa_ref[:, 3 * D_A:] = _dot(h_ref[...], win_ref[:, OFF_F:OFF_F + LANES])
    abuf_ref[HALO + r0:HALO + r0 + TM, :] = pa_ref[:, 0:D_A] * _sigmoid(pa_ref[:, D_A:2 * D_A])
    yield

    def mxu_tile(t):
        def fn():
            lo = t * TILE_COLS
            src = lo if lo < R_B else lo + 3 * D_A
            res = _dot(h_ref[...], win_ref[:, src:src + TILE_COLS])
            pbig_ref[:, lo:lo + TILE_COLS] = res
            tokens[t] = res[:SUBLANES, :LANES]
        return (MXU_TILE_COST, fn)

    tokens = {}
    n_tiles = W_REST // TILE_COLS
    n_units = (TM // CONV_ROWS) * (D_A // LANES)

    def conv_tasks(sub):
        rows = sub * CONV_ROWS
        units = []

        def unit(cb):
            k = sub * (D_A // LANES) + cb
            tile = min(n_tiles - 1, (k * n_tiles) // n_units)
            fn = lambda: units.append(_conv_unit(r0 + rows, cb, abuf_ref, convw_ref, tokens[tile]))
            return (CONV_UNIT_COST, fn, tile + 1)

        def epilogue():
            act_ref[rows:rows + CONV_ROWS, :] = _conv_epilogue(
                units, pa_ref[rows:rows + CONV_ROWS, 2 * D_A:3 * D_A], convb_ref, cng_ref, cnb_ref)

        return [unit(cb) for cb in range(D_A // LANES)] + [(CONV_UNIT_COST, epilogue, 0)]

    def gate_task(t):
        def fn():
            cs = slice(t * TILE_COLS, (t + 1) * TILE_COLS)
            pbig_ref[:, cs] = _sigmoid(pbig_ref[:, cs] + bg_ref[:, cs])
        return (GATE_TILE_COST, fn, t + 1)

    assert R_B % TILE_COLS == 0 and W_REST % TILE_COLS == 0
    n_gate_tiles = 3 * D_MODEL // TILE_COLS
    valu_tasks = [task for sub in range(TM // CONV_ROWS) for task in conv_tasks(sub)]
    valu_tasks += [gate_task(t) for t in range(n_gate_tiles)]
    _emit_interleaved([mxu_tile(t) for t in range(W_REST // TILE_COLS)], valu_tasks)
    yield

    def gate(idx):
        return pbig_ref[:, R_GATE + idx * D_MODEL:R_GATE + (idx + 1) * D_MODEL]

    lane = lax.broadcasted_iota(jnp.int32, (TM, LANES), 1)

    ya = _dot(act_ref[...], wa_ref[...])

    v = pbig_ref[:, R_B + D_B:R_B + 2 * D_B]
    v = (v * lax.rsqrt(jnp.mean(v * v, axis=-1, keepdims=True) + EPS)) * gng_ref[...]
    vb = v.astype(BF16)
    n_chunks = TM // CHUNK
    row = lax.broadcasted_iota(jnp.int32, (CHUNK, CHUNK), 0)
    col = lax.broadcasted_iota(jnp.int32, (CHUNK, CHUNK), 1)
    tril = col <= row
    mixed_g = []
    for g in range(N_GROUPS_B):
        gs = slice(g * CHUNK, (g + 1) * CHUNK)
        vg = jnp.concatenate([vb[n * CHUNK:(n + 1) * CHUNK, gs] for n in range(n_chunks)], axis=1)
        wsg = jnp.where(tril, ws_ref[g], 0.0).astype(BF16)
        mixed_g.append(_dot(wsg, vg) + bst_ref[:, g:g + 1])

    qk = pbig_ref[:, R_C:R_C + 2 * D_C]
    ssq = _dot((qk * qk).astype(BF16), esum_ref[...])

    z = pa_ref[:, 3 * D_A:] + bf_ref[...]
    logf = jnp.minimum(z, 0.0) - jnp.log1p(jnp.exp(-jnp.abs(z)))
    logf = jnp.where(lane < N_HEADS, logf, 0.0)
    l_hi, l_mid, l_lo = _split3(logf)
    l_split = l_hi + pltpu.roll(l_mid, 8, 1) + pltpu.roll(l_lo, 16, 1)
    trow = lax.broadcasted_iota(jnp.int32, (TM, TM), 0)
    tcol = lax.broadcasted_iota(jnp.int32, (TM, TM), 1)
    tri = jnp.where(tcol <= trow, 1.0, 0.0).astype(BF16)
    r = _dot(tri, l_split.astype(BF16))

    mab = gate(0) * ya

    rs = lax.rsqrt(ssq * (1.0 / HEAD_DIM) + EPS)
    rs_hi = rs.astype(BF16).astype(F32)
    rs_lo = (rs - rs_hi).astype(BF16).astype(F32)
    rs_split = jnp.where(lane < 16, rs_hi, jnp.where(lane < 32, pltpu.roll(rs_lo, 16, 1), 0.0))
    rs_b = _dot(rs_split.astype(BF16), ebc_ref[...])

    c = r + pltpu.roll(r, LANES - 8, 1) + pltpu.roll(r, LANES - 16, 1)
    c = jnp.where(lane < N_HEADS, c, 0.0) + ccar_ref[...]
    ccar_ref[...] = c[TM - 1:TM, :]
    c_hi, c_mid, c_lo = _split3(c * LOG2E)
    c_split = c_hi + pltpu.roll(c_mid, 8, 1) + pltpu.roll(c_lo, 16, 1)
    c_split = jnp.where(lane == 24, 1.0, c_split)
    aug = _dot(c_split.astype(BF16), place_ref[...])

    mixed = jnp.concatenate(
        [jnp.concatenate([mixed_g[g][:, n * CHUNK:(n + 1) * CHUNK] for g in range(N_GROUPS_B)], axis=1)
         for n in range(n_chunks)], axis=0)
    u = pbig_ref[:, R_B:R_B + D_B]
    act_b = ((u * mixed) * _silu(pbig_ref[:, R_B + 2 * D_B:R_B + 3 * D_B])).astype(BF16)
    yb = _dot(act_b, wb_ref[...])

    qkn = (qk * rs_b) * qkg_ref[...]
    vv = pbig_ref[:, R_C + 2 * D_C:R_C + 3 * D_C]
    scz_ref[0, out_rows] = _silu(pbig_ref[:, R_C + 3 * D_C:R_C + 4 * D_C]).astype(BF16)
    gc_ref[0, out_rows] = gate(2).astype(BF16)
    for hd in range(N_HEADS):
        p = hd // 2
        own = (lane < HEAD_DIM) if hd % 2 == 0 else (lane >= HEAD_DIM)
        one_lane = HEAD_DIM if hd % 2 == 0 else 0
        q_pair = qkn[:, p * LANES:(p + 1) * LANES]
        k_pair = qkn[:, D_C + p * LANES:D_C + (p + 1) * LANES]
        v_pair = vv[:, p * LANES:(p + 1) * LANES]
        qp_ref[0, hd, out_rows] = (jnp.where(own, q_pair, 0.0) + aug[:, hd * LANES:(hd + 1) * LANES]).astype(BF16)
        kp_ref[0, hd, out_rows] = (jnp.where(own, k_pair, 0.0)
                         + aug[:, (N_HEADS + hd) * LANES:(N_HEADS + hd + 1) * LANES]).astype(BF16)
        vp_ref[0, hd, out_rows] = jnp.where(own, v_pair, jnp.where(lane == one_lane, 1.0, 0.0)).astype(BF16)

    mab_ref[0, out_rows] = (mab + gate(1) * yb).astype(BF16)


def _proj_kernel(*refs):
    ins_outs, (abuf_ref, ccar_ref, h_ref, pa_ref, pbig_ref, act_ref) = refs[:-6], refs[-6:]
    i = pl.program_id(1)

    @pl.when(i == 0)
    def _():
        abuf_ref[0:HALO, :] = jnp.zeros((HALO, D_A), F32)
        ccar_ref[...] = jnp.zeros((1, LANES), F32)

    stages = [_proj_rows(g * TM, *ins_outs, abuf_ref, ccar_ref, h_ref.at[g], pa_ref.at[g], pbig_ref.at[g],
                         act_ref.at[g]) for g in range(ROW_GROUPS)]
    for g in range(ROW_GROUPS):
        next(stages[g])
        next(stages[g])
    for g in range(ROW_GROUPS):
        next(stages[g], None)
    abuf_ref[0:HALO, :] = abuf_ref[ROW_GROUPS * TM:ROW_GROUPS * TM + HALO, :]


def _const_spec(shape):
    nd = len(shape)
    return pl.BlockSpec(shape, lambda b, i, _nd=nd: (0,) * _nd, pipeline_mode=pl.Buffered(1))


def _proj_call(layer, x, ng, win, bg, convw, convb, cng, cnb, wa, gng, ws, bst, wb, qkg, bf, esum, ebc, place):
    bsz, seq, _ = x.shape
    step_rows = ROW_GROUPS * TM
    grid = (bsz, seq // step_rows)
    consts = (ng, win, bg, convw, convb, cng, cnb, wa, gng, ws, bst, wb, qkg, bf, esum, ebc, place)
    const_specs = [_const_spec(c.shape) for c in consts]
    const_specs[1] = pl.BlockSpec((None,) + win.shape[1:], lambda b, i: (layer, 0, 0),
                                  pipeline_mode=pl.Buffered(1))
    row_spec = lambda w: pl.BlockSpec((1, step_rows, w), lambda b, i: (b, i, 0))
    head_spec = pl.BlockSpec((1, N_HEADS, step_rows, LANES), lambda b, i: (b, 0, i, 0))
    head_shape = jax.ShapeDtypeStruct((bsz, N_HEADS, seq, LANES), BF16)
    return pl.pallas_call(
        _proj_kernel,
        grid=grid,
        in_specs=[row_spec(D_MODEL)] + const_specs,
        out_specs=[row_spec(D_MODEL), row_spec(D_MODEL), head_spec, head_spec, head_spec, row_spec(D_C)],
        out_shape=[jax.ShapeDtypeStruct((bsz, seq, D_MODEL), BF16),
                   jax.ShapeDtypeStruct((bsz, seq, D_MODEL), BF16),
                   head_shape, head_shape, head_shape,
                   jax.ShapeDtypeStruct((bsz, seq, D_C), BF16)],
        scratch_shapes=[pltpu.VMEM((step_rows + HALO, D_A), F32), pltpu.VMEM((1, LANES), F32),
                        pltpu.VMEM((ROW_GROUPS, TM, D_MODEL), BF16), pltpu.VMEM((ROW_GROUPS, TM, W_HEAD), F32),
                        pltpu.VMEM((ROW_GROUPS, TM, W_REST), F32),
                        pltpu.VMEM((ROW_GROUPS, TM, D_A), BF16)],
        compiler_params=pltpu.CompilerParams(
            dimension_semantics=("arbitrary", "arbitrary"), vmem_limit_bytes=VMEM_LIMIT),
        name="proj",
    )(x, *consts)


def _attn_kernel(q_ref, k_ref, v_ref, o_ref, sa_ref, sb_ref):
    i = pl.program_id(2)
    units = [(hh, st) for hh in range(2) for st in range(TQ // TQS)]
    qs = [q_ref[0, hh, st * TQS:(st + 1) * TQS, :] for hh, st in units]

    per_tile = TQ // TK
    all_units = tuple(range(len(units)))
    seeing = [tuple(u for u in all_units if (units[u][1] + 1) * TQS > d * TK) for d in range(per_tile)]
    bufs = (sa_ref, sb_ref)

    def key_rows(blk):
        return slice(blk * TK, (blk + 1) * TK)

    def qk(blk, s_ref, active):
        for u in active:
            s_ref[u] = lax.dot_general(k_ref[0, units[u][0], key_rows(blk), :], qs[u],
                                       (((1,), (1,)), ((), ())), preferred_element_type=F32)

    def softmax_pv(blk, s_ref, carry, active, diag):
        def scores(u):
            s = s_ref[u]
            if diag is not None:
                kpos = lax.broadcasted_iota(jnp.int32, (TK, TQS), 0) + diag * TK
                qpos = lax.broadcasted_iota(jnp.int32, (TK, TQS), 1) + units[u][1] * TQS
                s = jnp.where(kpos <= qpos, s, MASK_VALUE)
            return s

        m_new = {u: jnp.maximum(carry[u][0], jnp.max(scores(u), axis=0, keepdims=True)) for u in active}
        probs = {u: jnp.exp2(scores(u) - m_new[u]).astype(BF16) for u in active}
        out = list(carry)
        for u in active:
            pv = lax.dot_general(v_ref[0, units[u][0], key_rows(blk), :], probs[u],
                                 (((0,), (0,)), ((), ())), preferred_element_type=F32)
            out[u] = (m_new[u], jnp.exp2(carry[u][0] - m_new[u]) * carry[u][1] + pv)
        return tuple(out)

    def tile(n_full):
        def users(blk):
            return all_units if blk < n_full else seeing[blk - n_full]

        n_blocks = n_full + per_tile
        carry = tuple((jnp.full((1, TQS), MASK_VALUE, F32), jnp.zeros((LANES, TQS), F32)) for _ in units)
        qk(0, bufs[0], users(0))
        for blk in range(n_blocks):
            if blk + 1 < n_blocks:
                qk(blk + 1, bufs[(blk + 1) % 2], users(blk + 1))
            carry = softmax_pv(blk, bufs[blk % 2], carry, users(blk), blk - n_full if blk >= n_full else None)

        n_st = TQ // TQS
        acc0 = jnp.concatenate([carry[st][1] for st in range(n_st)], axis=1)
        acc1 = jnp.concatenate([carry[n_st + st][1] for st in range(n_st)], axis=1)
        feat = lax.broadcasted_iota(jnp.int32, (LANES, TQ), 0)
        o_t = jnp.where(feat < HEAD_DIM, acc0 / acc0[HEAD_DIM:HEAD_DIM + 1, :], acc1 / acc1[0:1, :])
        o_ref[0] = o_t.T.astype(BF16)

    for case in range(k_ref.shape[2] // TQ):
        pl.when(i == case)(functools.partial(tile, case * per_tile))


def _attn_call(qp, kp, vp):
    bsz, _, seq, _ = qp.shape
    grid = (bsz, N_HEADS // 2, seq // TQ)
    kv_spec = pl.BlockSpec((1, 2, seq, LANES), lambda b, p, i: (b, p, 0, 0))
    return pl.pallas_call(
        _attn_kernel,
        grid=grid,
        in_specs=[pl.BlockSpec((1, 2, TQ, LANES), lambda b, p, i: (b, p, i, 0)), kv_spec, kv_spec],
        out_specs=pl.BlockSpec((1, TQ, LANES), lambda b, p, i: (b, i, p)),
        out_shape=jax.ShapeDtypeStruct((bsz, seq, D_C), BF16),
        scratch_shapes=[pltpu.VMEM((2 * (TQ // TQS), TK, TQS), F32) for _ in range(2)],
        compiler_params=pltpu.CompilerParams(
            dimension_semantics=("arbitrary", "arbitrary", "arbitrary"), vmem_limit_bytes=VMEM_LIMIT),
        name="attn",
    )(qp, kp, vp)


def _merge_kernel(x_ref, mab_ref, gc_ref, o_ref, scz_ref, wc_ref, wout_ref, out_ref):
    act_c = (o_ref[0].astype(F32) * scz_ref[0].astype(F32)).astype(BF16)
    merged = mab_ref[0].astype(F32) + gc_ref[0].astype(F32) * _dot(act_c, wc_ref[...])
    out_ref[0] = x_ref[0] + _dot(merged.astype(BF16), wout_ref[...])


def _merge_call(x, mab, gc, o, scz, wc, wout):
    bsz, seq, _ = x.shape
    row_spec = lambda w: pl.BlockSpec((1, TM_MERGE, w), lambda b, i: (b, i, 0))
    return pl.pallas_call(
        _merge_kernel,
        grid=(bsz, seq // TM_MERGE),
        in_specs=[row_spec(D_MODEL), row_spec(D_MODEL), row_spec(D_MODEL), row_spec(D_C), row_spec(D_C),
                  _const_spec(wc.shape), _const_spec(wout.shape)],
        out_specs=row_spec(D_MODEL),
        out_shape=jax.ShapeDtypeStruct(x.shape, F32),
        compiler_params=pltpu.CompilerParams(
            dimension_semantics=("arbitrary", "arbitrary"), vmem_limit_bytes=VMEM_LIMIT),
        name="merge",
    )(x, mab, gc, o, scz, wc, wout)


def kernel(x, norm_g, w_in, b_gate, conv_w, conv_b, conv_norm_g, conv_norm_b, w_a, gmlp_norm_g,
           w_s, b_s, w_b, q_norm_g, k_norm_g, b_f, w_c, w_out):
    depth = w_in.shape[0]
    esum = jnp.asarray(_head_sum_matrix(), BF16)
    ebc = jnp.asarray(_head_bcast_matrix(), BF16)
    place = jnp.asarray(_decay_place_matrix(), BF16)
    win = jnp.pad(w_in.astype(BF16), ((0, 0), (0, 0), (0, N_IN_PAD - N_IN)))
    qkg = jnp.concatenate([jnp.tile(q_norm_g, (1, N_HEADS)) * QK_SCALE, jnp.tile(k_norm_g, (1, N_HEADS))], axis=1)
    bf = jnp.pad(b_f, ((0, 0), (0, LANES - N_HEADS)))
    row = lambda a, l: a[l][None, :]
    for l in range(depth):
        mab, gc, qp, kp, vp, scz = _proj_call(
            l, x, row(norm_g, l), win, row(b_gate, l), conv_w[l], row(conv_b, l), row(conv_norm_g, l),
            row(conv_norm_b, l), w_a[l].astype(BF16), row(gmlp_norm_g, l), w_s[l], b_s[l].T,
            w_b[l].astype(BF16), row(qkg, l), row(bf, l), esum, ebc, place)
        o = _attn_call(qp, kp, vp)
        x = _merge_call(x, mab, gc, o, scz, w_c[l].astype(BF16), w_out[l].astype(BF16))
    return x
```

```python
import functools
import math

import numpy as np
import jax
import jax.numpy as jnp
from jax import lax
from jax.experimental import pallas as pl
from jax.experimental.pallas import tpu as pltpu

D_MODEL = 1024
D_A = 512
D_B = 512
N_HEADS = 8
HEAD_DIM = 64
D_C = N_HEADS * HEAD_DIM
CONV_WIDTH = 31
CHUNK = 128
N_GROUPS_B = 4
EPS = 1e-6
N_IN = 3 * D_MODEL + 3 * D_A + 3 * D_B + 4 * D_C + N_HEADS

LANES = 128
SUBLANES = 8
MXU_COLS = 256
N_IN_PAD = ((N_IN + LANES - 1) // LANES) * LANES
TILE_COLS = MXU_COLS
OFF_GATE = 0
OFF_A = 3 * D_MODEL
OFF_B = OFF_A + 3 * D_A
OFF_C = OFF_B + 3 * D_B
OFF_F = OFF_C + 4 * D_C
W_HEAD = 3 * D_A + LANES
W_REST = 3 * D_MODEL + 3 * D_B + 4 * D_C
R_GATE = 0
R_B = R_GATE + 3 * D_MODEL
R_C = R_B + 3 * D_B

HALO = 32
CONV_ROWS = 64
TM = 256
ROW_GROUPS = 2
TM_MERGE = 1024
TQ = 1024
TK = 256
TQS = 256
MXU_TILE_COST = 130
CONV_UNIT_COST = 235
GATE_TILE_COST = 100
LOG2E = math.log2(math.e)
QK_SCALE = LOG2E / math.sqrt(HEAD_DIM)
MASK_VALUE = -1e30
VMEM_LIMIT = 58 * 1024 * 1024

F32 = jnp.float32
BF16 = jnp.bfloat16


def _dot(a, b):
    return jnp.dot(a, b, preferred_element_type=F32)


def _sigmoid(x):
    return 1.0 / (1.0 + jnp.exp2(x * (-LOG2E)))


def _silu(x):
    return x * _sigmoid(x)


def _split3(x):
    hi = x.astype(BF16).astype(F32)
    r = x - hi
    mid = r.astype(BF16).astype(F32)
    lo = (r - mid).astype(BF16).astype(F32)
    return hi, mid, lo


def _head_sum_matrix():
    m = np.zeros((2 * D_C, LANES), np.float32)
    for h in range(N_HEADS):
        m[h * HEAD_DIM:(h + 1) * HEAD_DIM, h] = 1.0
        m[D_C + h * HEAD_DIM:D_C + (h + 1) * HEAD_DIM, N_HEADS + h] = 1.0
    return m


def _head_bcast_matrix():
    m = np.zeros((LANES, 2 * D_C), np.float32)
    for part in range(2):
        for h in range(N_HEADS):
            m[16 * part + h, h * HEAD_DIM:(h + 1) * HEAD_DIM] = 1.0
            m[16 * part + N_HEADS + h, D_C + h * HEAD_DIM:D_C + (h + 1) * HEAD_DIM] = 1.0
    return m


def _aug_offset(h):
    return HEAD_DIM if h % 2 == 0 else 0


def _decay_place_matrix():
    m = np.zeros((LANES, 2 * N_HEADS * LANES), np.float32)
    for h in range(N_HEADS):
        qb = h * LANES + _aug_offset(h)
        kb = N_HEADS * LANES + h * LANES + _aug_offset(h)
        for part in range(3):
            m[8 * part + h, qb + part] = 1.0
            m[24, qb + 3 + part] = 1.0
            m[24, kb + part] = 1.0
            m[8 * part + h, kb + 3 + part] = -1.0
    return m


def _zero_after(token):
    bits = pltpu.bitcast(token, jnp.uint32)
    return lax.shift_right_logical(lax.shift_right_logical(bits, jnp.uint32(16)), jnp.uint32(16)).astype(F32)


def _conv_unit(base, cb, abuf_ref, convw_ref, token):
    cs = slice(cb * LANES, (cb + 1) * LANES)
    win = abuf_ref[pl.ds(base, CONV_ROWS + HALO), cs]
    win = jnp.concatenate([win[:SUBLANES] + _zero_after(token), win[SUBLANES:]], axis=0)
    conv = None
    for r in range(SUBLANES):
        part = None
        rows = CONV_ROWS + (SUBLANES if r else 0)
        for j in range(CONV_WIDTH):
            off = HALO - (CONV_WIDTH - 1) + j
            if off % SUBLANES != r:
                continue
            term = convw_ref[j:j + 1, cs] * win[off - r:off - r + rows, :]
            part = term if part is None else part + term
        part = part[r:r + CONV_ROWS, :]
        conv = part if conv is None else conv + part
    return conv


def _conv_epilogue(units, a_z, convb_ref, cng_ref, cnb_ref):
    conv = jnp.concatenate(units, axis=1) + convb_ref[...]
    mu = jnp.mean(conv, axis=-1, keepdims=True)
    xc = conv - mu
    var = jnp.mean(xc * xc, axis=-1, keepdims=True)
    ln = (xc * lax.rsqrt(var + EPS)) * cng_ref[...] + cnb_ref[...]
    return (_silu(ln) * _silu(a_z)).astype(BF16)


def _emit_interleaved(mxu_tasks, valu_tasks):
    mi = vi = 0
    mcost = vcost = 0.0
    while mi < len(mxu_tasks) or vi < len(valu_tasks):
        take_mxu = vi == len(valu_tasks) or (
            mi < len(mxu_tasks) and (mcost <= vcost or valu_tasks[vi][2] > mi))
        if take_mxu:
            cost, fn = mxu_tasks[mi]
            mi += 1
            mcost += cost
            fn()
        else:
            cost, fn, _ = valu_tasks[vi]
            vi += 1
            vcost += cost
            fn()


def _proj_rows(r0, x_ref, ng_ref, win_ref, bg_ref, convw_ref, convb_ref, cng_ref, cnb_ref, wa_ref,
               gng_ref, ws_ref, bst_ref, wb_ref, qkg_ref, bf_ref, esum_ref, ebc_ref, place_ref,
               mab_ref, gc_ref, qp_ref, kp_ref, vp_ref, scz_ref,
               abuf_ref, ccar_ref, h_ref, pa_ref, pbig_ref, act_ref):
    out_rows = slice(r0, r0 + TM)
    x = x_ref[0, out_rows]
    ms = jnp.mean(x * x, axis=-1, keepdims=True)
    h_ref[...] = ((x * lax.rsqrt(ms + EPS)) * ng_ref[...]).astype(BF16)

    pa_ref[:, 0:3 * D_A] = _dot(h_ref[...], win_ref[:, OFF_A:OFF_A + 3 * D_A])
    pa_ref[:, 3 * D_A:] = _dot(h_ref[...], win_ref[:, OFF_F:OFF_F + LANES])
    abuf_ref[HALO + r0:HALO + r0 + TM, :] = pa_ref[:, 0:D_A] * _sigmoid(pa_ref[:, D_A:2 * D_A])
    yield

    def mxu_tile(t):
        def fn():
            lo = t * TILE_COLS
            src = lo if lo < R_B else lo + 3 * D_A
            res = _dot(h_ref[...], win_ref[:, src:src + TILE_COLS])
            pbig_ref[:, lo:lo + TILE_COLS] = res
            tokens[t] = res[:SUBLANES, :LANES]
        return (MXU_TILE_COST, fn)

    tokens = {}
    n_tiles = W_REST // TILE_COLS
    n_units = (TM // CONV_ROWS) * (D_A // LANES)

    def conv_tasks(sub):
        rows = sub * CONV_ROWS
        units = []

        def unit(cb):
            k = sub * (D_A // LANES) + cb
            tile = min(n_tiles - 1, (k * n_tiles) // n_units)
            fn = lambda: units.append(_conv_unit(r0 + rows, cb, abuf_ref, convw_ref, tokens[tile]))
            return (CONV_UNIT_COST, fn, tile + 1)

        def epilogue():
            act_ref[rows:rows + CONV_ROWS, :] = _conv_epilogue(
                units, pa_ref[rows:rows + CONV_ROWS, 2 * D_A:3 * D_A], convb_ref, cng_ref, cnb_ref)

        return [unit(cb) for cb in range(D_A // LANES)] + [(CONV_UNIT_COST, epilogue, 0)]

    def gate_task(t):
        def fn():
            cs = slice(t * TILE_COLS, (t + 1) * TILE_COLS)
            pbig_ref[:, cs] = _sigmoid(pbig_ref[:, cs] + bg_ref[:, cs])
        return (GATE_TILE_COST, fn, t + 1)

    assert R_B % TILE_COLS == 0 and W_REST % TILE_COLS == 0
    n_gate_tiles = 3 * D_MODEL // TILE_COLS
    valu_tasks = [task for sub in range(TM // CONV_ROWS) for task in conv_tasks(sub)]
    valu_tasks += [gate_task(t) for t in range(n_gate_tiles)]
    _emit_interleaved([mxu_tile(t) for t in range(W_REST // TILE_COLS)], valu_tasks)
    yield

    def gate(idx):
        return pbig_ref[:, R_GATE + idx * D_MODEL:R_GATE + (idx + 1) * D_MODEL]

    lane = lax.broadcasted_iota(jnp.int32, (TM, LANES), 1)

    ya = _dot(act_ref[...], wa_ref[...])

    v = pbig_ref[:, R_B + D_B:R_B + 2 * D_B]
    v = (v * lax.rsqrt(jnp.mean(v * v, axis=-1, keepdims=True) + EPS)) * gng_ref[...]
    vb = v.astype(BF16)
    n_chunks = TM // CHUNK
    row = lax.broadcasted_iota(jnp.int32, (CHUNK, CHUNK), 0)
    col = lax.broadcasted_iota(jnp.int32, (CHUNK, CHUNK), 1)
    tril = col <= row
    mixed_g = []
    for g in range(N_GROUPS_B):
        gs = slice(g * CHUNK, (g + 1) * CHUNK)
        vg = jnp.concatenate([vb[n * CHUNK:(n + 1) * CHUNK, gs] for n in range(n_chunks)], axis=1)
        wsg = jnp.where(tril, ws_ref[g], 0.0).astype(BF16)
        mixed_g.append(_dot(wsg, vg) + bst_ref[:, g:g + 1])

    qk = pbig_ref[:, R_C:R_C + 2 * D_C]
    ssq = _dot((qk * qk).astype(BF16), esum_ref[...])

    z = pa_ref[:, 3 * D_A:] + bf_ref[...]
    logf = jnp.minimum(z, 0.0) - jnp.log1p(jnp.exp(-jnp.abs(z)))
    logf = jnp.where(lane < N_HEADS, logf, 0.0)
    l_hi, l_mid, l_lo = _split3(logf)
    l_split = l_hi + pltpu.roll(l_mid, 8, 1) + pltpu.roll(l_lo, 16, 1)
    trow = lax.broadcasted_iota(jnp.int32, (TM, TM), 0)
    tcol = lax.broadcasted_iota(jnp.int32, (TM, TM), 1)
    tri = jnp.where(tcol <= trow, 1.0, 0.0).astype(BF16)
    r = _dot(tri, l_split.astype(BF16))
    yield

    mab = gate(0) * ya

    rs = lax.rsqrt(ssq * (1.0 / HEAD_DIM) + EPS)
    rs_hi = rs.astype(BF16).astype(F32)
    rs_lo = (rs - rs_hi).astype(BF16).astype(F32)
    rs_split = jnp.where(lane < 16, rs_hi, jnp.where(lane < 32, pltpu.roll(rs_lo, 16, 1), 0.0))
    rs_b = _dot(rs_split.astype(BF16), ebc_ref[...])

    c = r + pltpu.roll(r, LANES - 8, 1) + pltpu.roll(r, LANES - 16, 1)
    c = jnp.where(lane < N_HEADS, c, 0.0) + ccar_ref[...]
    ccar_ref[...] = c[TM - 1:TM, :]
    c_hi, c_mid, c_lo = _split3(c * LOG2E)
    c_split = c_hi + pltpu.roll(c_mid, 8, 1) + pltpu.roll(c_lo, 16, 1)
    c_split = jnp.where(lane == 24, 1.0, c_split)
    aug = _dot(c_split.astype(BF16), place_ref[...])

    mixed = jnp.concatenate(
        [jnp.concatenate([mixed_g[g][:, n * CHUNK:(n + 1) * CHUNK] for g in range(N_GROUPS_B)], axis=1)
         for n in range(n_chunks)], axis=0)
    u = pbig_ref[:, R_B:R_B + D_B]
    act_b = ((u * mixed) * _silu(pbig_ref[:, R_B + 2 * D_B:R_B + 3 * D_B])).astype(BF16)
    yb = _dot(act_b, wb_ref[...])

    qkn = (qk * rs_b) * qkg_ref[...]
    vv = pbig_ref[:, R_C + 2 * D_C:R_C + 3 * D_C]
    scz_ref[0, out_rows] = _silu(pbig_ref[:, R_C + 3 * D_C:R_C + 4 * D_C]).astype(BF16)
    gc_ref[0, out_rows] = gate(2).astype(BF16)
    for hd in range(N_HEADS):
        p = hd // 2
        own = (lane < HEAD_DIM) if hd % 2 == 0 else (lane >= HEAD_DIM)
        one_lane = HEAD_DIM if hd % 2 == 0 else 0
        q_pair = qkn[:, p * LANES:(p + 1) * LANES]
        k_pair = qkn[:, D_C + p * LANES:D_C + (p + 1) * LANES]
        v_pair = vv[:, p * LANES:(p + 1) * LANES]
        qp_ref[0, hd, out_rows] = (jnp.where(own, q_pair, 0.0) + aug[:, hd * LANES:(hd + 1) * LANES]).astype(BF16)
        kp_ref[0, hd, out_rows] = (jnp.where(own, k_pair, 0.0)
                         + aug[:, (N_HEADS + hd) * LANES:(N_HEADS + hd + 1) * LANES]).astype(BF16)
        vp_ref[0, hd, out_rows] = jnp.where(own, v_pair, jnp.where(lane == one_lane, 1.0, 0.0)).astype(BF16)

    mab_ref[0, out_rows] = (mab + gate(1) * yb).astype(BF16)


def _proj_kernel(*refs):
    ins_outs, (abuf_ref, ccar_ref, h_ref, pa_ref, pbig_ref, act_ref) = refs[:-6], refs[-6:]
    i = pl.program_id(1)

    @pl.when(i == 0)
    def _():
        abuf_ref[0:HALO, :] = jnp.zeros((HALO, D_A), F32)
        ccar_ref[...] = jnp.zeros((1, LANES), F32)

    stages = [_proj_rows(g * TM, *ins_outs, abuf_ref, ccar_ref, h_ref.at[g], pa_ref.at[g], pbig_ref.at[g],
                         act_ref.at[g]) for g in range(ROW_GROUPS)]
    for g in range(ROW_GROUPS):
        next(stages[g])
        next(stages[g])
    for g in range(ROW_GROUPS):
        next(stages[g])
    for g in range(ROW_GROUPS):
        next(stages[g], None)
    abuf_ref[0:HALO, :] = abuf_ref[ROW_GROUPS * TM:ROW_GROUPS * TM + HALO, :]


def _const_spec(shape):
    nd = len(shape)
    return pl.BlockSpec(shape, lambda b, i, _nd=nd: (0,) * _nd, pipeline_mode=pl.Buffered(1))


def _proj_call(layer, x, ng, win, bg, convw, convb, cng, cnb, wa, gng, ws, bst, wb, qkg, bf, esum, ebc, place):
    bsz, seq, _ = x.shape
    step_rows = ROW_GROUPS * TM
    grid = (bsz, seq // step_rows)
    consts = (ng, win, bg, convw, convb, cng, cnb, wa, gng, ws, bst, wb, qkg, bf, esum, ebc, place)
    const_specs = [_const_spec(c.shape) for c in consts]
    const_specs[1] = pl.BlockSpec((None,) + win.shape[1:], lambda b, i: (layer, 0, 0),
                                  pipeline_mode=pl.Buffered(1))
    row_spec = lambda w: pl.BlockSpec((1, step_rows, w), lambda b, i: (b, i, 0))
    head_spec = pl.BlockSpec((1, N_HEADS, step_rows, LANES), lambda b, i: (b, 0, i, 0))
    head_shape = jax.ShapeDtypeStruct((bsz, N_HEADS, seq, LANES), BF16)
    return pl.pallas_call(
        _proj_kernel,
        grid=grid,
        in_specs=[row_spec(D_MODEL)] + const_specs,
        out_specs=[row_spec(D_MODEL), row_spec(D_MODEL), head_spec, head_spec, head_spec, row_spec(D_C)],
        out_shape=[jax.ShapeDtypeStruct((bsz, seq, D_MODEL), BF16),
                   jax.ShapeDtypeStruct((bsz, seq, D_MODEL), BF16),
                   head_shape, head_shape, head_shape,
                   jax.ShapeDtypeStruct((bsz, seq, D_C), BF16)],
        scratch_shapes=[pltpu.VMEM((step_rows + HALO, D_A), F32), pltpu.VMEM((1, LANES), F32),
                        pltpu.VMEM((ROW_GROUPS, TM, D_MODEL), BF16), pltpu.VMEM((ROW_GROUPS, TM, W_HEAD), F32),
                        pltpu.VMEM((ROW_GROUPS, TM, W_REST), F32),
                        pltpu.VMEM((ROW_GROUPS, TM, D_A), BF16)],
        compiler_params=pltpu.CompilerParams(
            dimension_semantics=("arbitrary", "arbitrary"), vmem_limit_bytes=VMEM_LIMIT),
        name="proj",
    )(x, *consts)


def _attn_kernel(q_ref, k_ref, v_ref, o_ref, sa_ref, sb_ref):
    i = pl.program_id(2)
    units = [(hh, st) for hh in range(2) for st in range(TQ // TQS)]
    qs = [q_ref[0, hh, st * TQS:(st + 1) * TQS, :] for hh, st in units]

    per_tile = TQ // TK
    all_units = tuple(range(len(units)))
    seeing = [tuple(u for u in all_units if (units[u][1] + 1) * TQS > d * TK) for d in range(per_tile)]
    bufs = (sa_ref, sb_ref)

    def key_rows(blk):
        return slice(blk * TK, (blk + 1) * TK)

    def qk(blk, s_ref, active):
        for u in active:
            s_ref[u] = lax.dot_general(k_ref[0, units[u][0], key_rows(blk), :], qs[u],
                                       (((1,), (1,)), ((), ())), preferred_element_type=F32)

    def softmax_pv(blk, s_ref, carry, active, diag):
        def scores(u):
            s = s_ref[u]
            if diag is not None:
                kpos = lax.broadcasted_iota(jnp.int32, (TK, TQS), 0) + diag * TK
                qpos = lax.broadcasted_iota(jnp.int32, (TK, TQS), 1) + units[u][1] * TQS
                s = jnp.where(kpos <= qpos, s, MASK_VALUE)
            return s

        m_new = {u: jnp.maximum(carry[u][0], jnp.max(scores(u), axis=0, keepdims=True)) for u in active}
        probs = {u: jnp.exp2(scores(u) - m_new[u]).astype(BF16) for u in active}
        out = list(carry)
        for u in active:
            pv = lax.dot_general(v_ref[0, units[u][0], key_rows(blk), :], probs[u],
                                 (((0,), (0,)), ((), ())), preferred_element_type=F32)
            out[u] = (m_new[u], jnp.exp2(carry[u][0] - m_new[u]) * carry[u][1] + pv)
        return tuple(out)

    def tile(n_full):
        def users(blk):
            return all_units if blk < n_full else seeing[blk - n_full]

        n_blocks = n_full + per_tile
        carry = tuple((jnp.full((1, TQS), MASK_VALUE, F32), jnp.zeros((LANES, TQS), F32)) for _ in units)
        qk(0, bufs[0], users(0))
        for blk in range(n_blocks):
            if blk + 1 < n_blocks:
                qk(blk + 1, bufs[(blk + 1) % 2], users(blk + 1))
            carry = softmax_pv(blk, bufs[blk % 2], carry, users(blk), blk - n_full if blk >= n_full else None)

        n_st = TQ // TQS
        acc0 = jnp.concatenate([carry[st][1] for st in range(n_st)], axis=1)
        acc1 = jnp.concatenate([carry[n_st + st][1] for st in range(n_st)], axis=1)
        feat = lax.broadcasted_iota(jnp.int32, (LANES, TQ), 0)
        o_t = jnp.where(feat < HEAD_DIM, acc0 / acc0[HEAD_DIM:HEAD_DIM + 1, :], acc1 / acc1[0:1, :])
        o_ref[0] = o_t.T.astype(BF16)

    for case in range(k_ref.shape[2] // TQ):
        pl.when(i == case)(functools.partial(tile, case * per_tile))


def _attn_call(qp, kp, vp):
    bsz, _, seq, _ = qp.shape
    grid = (bsz, N_HEADS // 2, seq // TQ)
    kv_spec = pl.BlockSpec((1, 2, seq, LANES), lambda b, p, i: (b, p, 0, 0))
    return pl.pallas_call(
        _attn_kernel,
        grid=grid,
        in_specs=[pl.BlockSpec((1, 2, TQ, LANES), lambda b, p, i: (b, p, i, 0)), kv_spec, kv_spec],
        out_specs=pl.BlockSpec((1, TQ, LANES), lambda b, p, i: (b, i, p)),
        out_shape=jax.ShapeDtypeStruct((bsz, seq, D_C), BF16),
        scratch_shapes=[pltpu.VMEM((2 * (TQ // TQS), TK, TQS), F32) for _ in range(2)],
        compiler_params=pltpu.CompilerParams(
            dimension_semantics=("arbitrary", "arbitrary", "arbitrary"), vmem_limit_bytes=VMEM_LIMIT),
        name="attn",
    )(qp, kp, vp)


def _merge_kernel(x_ref, mab_ref, gc_ref, o_ref, scz_ref, wc_ref, wout_ref, out_ref):
    act_c = (o_ref[0].astype(F32) * scz_ref[0].astype(F32)).astype(BF16)
    merged = mab_ref[0].astype(F32) + gc_ref[0].astype(F32) * _dot(act_c, wc_ref[...])
    out_ref[0] = x_ref[0] + _dot(merged.astype(BF16), wout_ref[...])


def _merge_call(x, mab, gc, o, scz, wc, wout):
    bsz, seq, _ = x.shape
    row_spec = lambda w: pl.BlockSpec((1, TM_MERGE, w), lambda b, i: (b, i, 0))
    return pl.pallas_call(
        _merge_kernel,
        grid=(bsz, seq // TM_MERGE),
        in_specs=[row_spec(D_MODEL), row_spec(D_MODEL), row_spec(D_MODEL), row_spec(D_C), row_spec(D_C),
                  _const_spec(wc.shape), _const_spec(wout.shape)],
        out_specs=row_spec(D_MODEL),
        out_shape=jax.ShapeDtypeStruct(x.shape, F32),
        compiler_params=pltpu.CompilerParams(
            dimension_semantics=("arbitrary", "arbitrary"), vmem_limit_bytes=VMEM_LIMIT),
        name="merge",
    )(x, mab, gc, o, scz, wc, wout)


def kernel(x, norm_g, w_in, b_gate, conv_w, conv_b, conv_norm_g, conv_norm_b, w_a, gmlp_norm_g,
           w_s, b_s, w_b, q_norm_g, k_norm_g, b_f, w_c, w_out):
    depth = w_in.shape[0]
    esum = jnp.asarray(_head_sum_matrix(), BF16)
    ebc = jnp.asarray(_head_bcast_matrix(), BF16)
    place = jnp.asarray(_decay_place_matrix(), BF16)
    win = jnp.pad(w_in.astype(BF16), ((0, 0), (0, 0), (0, N_IN_PAD - N_IN)))
    qkg = jnp.concatenate([jnp.tile(q_norm_g, (1, N_HEADS)) * QK_SCALE, jnp.tile(k_norm_g, (1, N_HEADS))], axis=1)
    bf = jnp.pad(b_f, ((0, 0), (0, LANES - N_HEADS)))
    row = lambda a, l: a[l][None, :]
    for l in range(depth):
        mab, gc, qp, kp, vp, scz = _proj_call(
            l, x, row(norm_g, l), win, row(b_gate, l), conv_w[l], row(conv_b, l), row(conv_norm_g, l),
            row(conv_norm_b, l), w_a[l].astype(BF16), row(gmlp_norm_g, l), w_s[l], b_s[l].T,
            w_b[l].astype(BF16), row(qkg, l), row(bf, l), esum, ebc, place)
        o = _attn_call(qp, kp, vp)
        x = _merge_call(x, mab, gc, o, scz, w_c[l].astype(BF16), w_out[l].astype(BF16))
    return x
```
